```python
import math
import jax, jax.numpy as jnp
from jax import lax
import numpy as np

D_MODEL = 1024
BATCH = 2
SEQ = 8192
DEPTH = 1
DEC_BATCH = 32
DEC_SEQ = 4
PAST_LEN = 16384
PAGE_SIZE = 128

HEAD_DIM = 64
MOBA_HEADS = 8
MOBA_BLOCK = 256
MOBA_TOPK = 3
MOBA_QCHUNK = 64
DIFF_HEADS = 4
DIFF_VDIM = 2 * HEAD_DIM
DIFF_QBLOCK = 128
MOBA_W = MOBA_HEADS * HEAD_DIM
DIFF_QK_W = DIFF_HEADS * 2 * HEAD_DIM
DIFF_V_W = DIFF_HEADS * DIFF_VDIM
MIX_WIDTH = MOBA_W + DIFF_V_W
IN_WIDTH = 3 * MOBA_W + 2 * DIFF_QK_W + DIFF_V_W
IN_SPLITS = (MOBA_W, 2 * MOBA_W, 3 * MOBA_W, 3 * MOBA_W + DIFF_QK_W, 3 * MOBA_W + 2 * DIFF_QK_W)
ATTN_SCALE = HEAD_DIM ** -0.5
REL_BUCKETS = 32
REL_MAX_EXACT = 16
REL_MAX_DIST = 128
N_BIAS_HEADS = MOBA_HEADS + DIFF_HEADS
PEER_HEADS = 8
PEER_NKEYS = 128
PEER_EXPERTS = PEER_NKEYS * PEER_NKEYS
PEER_KEY_DIM = 256
PEER_HALF = PEER_KEY_DIM // 2
PEER_TOPK = 16
PEER_CHUNK = 256
PLE_DIM = 256
RMS_EPS = 1e-6

kernel_name = 'hybrid_moba_diff_peer_step'


def rmsnorm(x, g):
    xf = x.astype(jnp.float32)
    y = xf * lax.rsqrt(jnp.mean(xf * xf, axis=-1, keepdims=True) + RMS_EPS)
    return (y * g.astype(jnp.float32)).astype(x.dtype)


def rel_bucket(rel):
    n = jnp.maximum(rel, 0)
    nf = jnp.maximum(n, 1).astype(jnp.float32)
    large = REL_MAX_EXACT + (jnp.log(nf / REL_MAX_EXACT) / math.log(REL_MAX_DIST / REL_MAX_EXACT)
                             * (REL_BUCKETS - REL_MAX_EXACT)).astype(jnp.int32)
    large = jnp.minimum(large, REL_BUCKETS - 1)
    return jnp.where(n < REL_MAX_EXACT, n, large)


def project_heads(xn, w_in, g_qm, g_km, g_qd, g_kd):
    z = xn @ w_in
    lead = z.shape[:-1]
    qm, km, vm, qd, kd, vd = jnp.split(z, IN_SPLITS, axis=-1)
    qm = rmsnorm(qm.reshape(*lead, MOBA_HEADS, HEAD_DIM), g_qm)
    km = rmsnorm(km.reshape(*lead, MOBA_HEADS, HEAD_DIM), g_km)
    vm = vm.reshape(*lead, MOBA_HEADS, HEAD_DIM)
    qd = rmsnorm(qd.reshape(*lead, DIFF_HEADS, 2, HEAD_DIM), g_qd)
    kd = rmsnorm(kd.reshape(*lead, DIFF_HEADS, 2, HEAD_DIM), g_kd)
    vd = vd.reshape(*lead, DIFF_HEADS, DIFF_VDIM)
    return qm, km, vm, qd, kd, vd


def to_blocks(k):
    n = k.shape[0]
    nb = -(-n // MOBA_BLOCK)
    k = jnp.pad(k, ((0, nb * MOBA_BLOCK - n), (0, 0), (0, 0)))
    return jnp.transpose(k.reshape(nb, MOBA_BLOCK, *k.shape[1:]), (2, 0, 1, 3))


def moba_attend(q, kb, vb, kmean, qpos, bias_m):
    tq, nh = q.shape[0], q.shape[1]
    nb = kb.shape[1]
    n_sel = min(MOBA_TOPK, nb)
    qblk = qpos // MOBA_BLOCK
    gate = jnp.einsum('thd,hnd->thn', q, kmean).astype(jnp.float32)
    past = jnp.arange(nb)[None, None, :] < qblk[:, None, None]
    gate = jnp.where(past, gate, -jnp.inf)
    _, top_idx = lax.top_k(gate, n_sel)
    rank_ok = jnp.arange(n_sel)[None, :] < qblk[:, None]
    own = jnp.broadcast_to(qblk[:, None, None], (tq, nh, 1))
    sel = jnp.concatenate([top_idx, own], axis=-1)
    sel_ok = jnp.concatenate([rank_ok, jnp.ones((tq, 1), dtype=bool)], axis=-1)
    hidx = jnp.arange(nh)[None, :, None]
    kg = kb[hidx, sel]
    vg = vb[hidx, sel]
    kpos = sel[..., None] * MOBA_BLOCK + jnp.arange(MOBA_BLOCK)
    rel = qpos[:, None, None, None] - kpos
    mask = sel_ok[:, None, :, None] & (rel >= 0)
    bias = bias_m[rel_bucket(rel), hidx[..., None]]
    logits = jnp.einsum('thd,thjnd->thjn', q, kg).astype(jnp.float32) * ATTN_SCALE + bias
    logits = jnp.where(mask, logits, -jnp.inf)
    w = jax.nn.softmax(logits.reshape(tq, nh, -1), axis=-1).astype(vg.dtype)
    return jnp.einsum('thm,thmd->thd', w, vg.reshape(tq, nh, -1, vg.shape[-1]))


def diff_attend(q, k, v, qpos, kpos, lam, bias_d):
    logits = jnp.einsum('thpd,lhpd->hptl', q, k).astype(jnp.float32) * ATTN_SCALE
    rel = qpos[:, None] - kpos[None, :]
    bias = jnp.moveaxis(bias_d[rel_bucket(rel)], -1, 0)
    logits = jnp.where(rel >= 0, logits + bias[:, None], -jnp.inf)
    a = jax.nn.softmax(logits, axis=-1)
    attn = a[:, 0] - lam * a[:, 1]
    return jnp.einsum('htl,lhe->the', attn.astype(v.dtype), v)


def prompt_mixers(qm, km, vm, qd, kd, vd, lam, bias_m, bias_d):
    b, s = qm.shape[0], qm.shape[1]
    pos = jnp.arange(s, dtype=jnp.int32)
    kb = jax.vmap(to_blocks)(km)
    vb = jax.vmap(to_blocks)(vm)
    kmean = jnp.mean(kb, axis=3)
    moba_b = jax.vmap(moba_attend, in_axes=(0, 0, 0, 0, None, None))
    nqc = s // MOBA_QCHUNK
    qm_c = jnp.swapaxes(qm.reshape(b, nqc, MOBA_QCHUNK, MOBA_HEADS, HEAD_DIM), 0, 1)
    om = lax.map(lambda a: moba_b(a[0], kb, vb, kmean, a[1], bias_m),
                 (qm_c, pos.reshape(nqc, MOBA_QCHUNK)))
    om = jnp.swapaxes(om, 0, 1).reshape(b, s, MOBA_HEADS, HEAD_DIM)
    diff_b = jax.vmap(diff_attend, in_axes=(0, 0, 0, None, None, None, None))
    nqb = s // DIFF_QBLOCK
    qd_c = jnp.swapaxes(qd.reshape(b, nqb, DIFF_QBLOCK, DIFF_HEADS, 2, HEAD_DIM), 0, 1)
    od = lax.map(lambda a: diff_b(a[0], kd, vd, a[1], pos, lam, bias_d),
                 (qd_c, pos.reshape(nqb, DIFF_QBLOCK)))
    od = jnp.swapaxes(od, 0, 1).reshape(b, s, DIFF_HEADS, DIFF_VDIM)
    return om, od


def sample_mixers(qm, km, vm, qd, kd, vd, page_table, ck_m, cv_m, ck_d, cv_d, layer, lam, bias_m, bias_d):
    past_len = page_table.shape[1] * PAGE_SIZE
    tn = qm.shape[1]
    qpos = past_len + jnp.arange(tn, dtype=jnp.int32)
    kpos = jnp.arange(past_len + tn, dtype=jnp.int32)

    def gather_past(pool, pt):
        rows = pool[layer, pt]
        return rows.reshape(past_len, *rows.shape[2:])

    def one(a):
        qm1, km1, vm1, qd1, kd1, vd1, pt = a
        k_m = jnp.concatenate([gather_past(ck_m, pt), km1], axis=0)
        v_m = jnp.concatenate([gather_past(cv_m, pt), vm1], axis=0)
        k_d = jnp.concatenate([gather_past(ck_d, pt), kd1], axis=0)
        v_d = jnp.concatenate([gather_past(cv_d, pt), vd1], axis=0)
        kb = to_blocks(k_m)
        vb = to_blocks(v_m)
        om1 = moba_attend(qm1, kb, vb, jnp.mean(kb, axis=2), qpos, bias_m)
        od1 = diff_attend(qd1, k_d, v_d, qpos, kpos, lam, bias_d)
        return om1, od1

    return lax.map(one, (qm, km, vm, qd, kd, vd, page_table))


def merge_heads(om, od, g_sub, lam_init, w_out):
    od = rmsnorm(od, g_sub) * (1.0 - lam_init)
    lead = om.shape[:-2]
    y = jnp.concatenate([om.reshape(*lead, MOBA_W), od.reshape(*lead, DIFF_V_W)], axis=-1)
    return y @ w_out


def peer_chunk(xc, w_q, sub_keys, u, v):
    c = xc.shape[0]
    q = (xc @ w_q).reshape(c, PEER_HEADS, 2, PEER_HALF)
    s = jnp.einsum('chpd,hpkd->chpk', q, sub_keys).astype(jnp.float32)
    s1, i1 = lax.top_k(s[:, :, 0], PEER_TOPK)
    s2, i2 = lax.top_k(s[:, :, 1], PEER_TOPK)
    cand_s = (s1[..., :, None] + s2[..., None, :]).reshape(c, PEER_HEADS, -1)
    cand_i = (i1[..., :, None] * PEER_NKEYS + i2[..., None, :]).reshape(c, PEER_HEADS, -1)
    top_s, top_j = lax.top_k(cand_s, PEER_TOPK)
    eidx = jnp.take_along_axis(cand_i, top_j, axis=-1)
    g = jax.nn.softmax(top_s, axis=-1)
    hid = jax.nn.gelu(jnp.einsum('chkd,cd->chk', u[eidx], xc).astype(jnp.float32), approximate=False)
    coef = (g * hid).astype(xc.dtype)
    return jnp.einsum('chk,chkd->cd', coef, v[eidx])


def peer_ffn(x, w_q, sub_keys, u, v):
    lead = x.shape[:-1]
    xt = x.reshape(-1, x.shape[-1])
    t = xt.shape[0]
    nc = -(-t // PEER_CHUNK)
    xt = jnp.pad(xt, ((0, nc * PEER_CHUNK - t), (0, 0)))
    out = lax.map(lambda xc: peer_chunk(xc, w_q, sub_keys, u, v), xt.reshape(nc, PEER_CHUNK, -1))
    return out.reshape(nc * PEER_CHUNK, -1)[:t].reshape(*lead, -1)


def channel_and_ple(h, p, g_ffn, w_pq, sub_keys, u, v, g_ple, w_pg, w_pp):
    h = h + peer_ffn(rmsnorm(h, g_ffn), w_pq, sub_keys, u, v)
    gate = jax.nn.sigmoid(rmsnorm(h, g_ple) @ w_pg)
    return h + gate * (p @ w_pp)


def setup_inputs(seed: int = 0) -> dict:
    key = jax.random.key(seed)
    ks = jax.random.split(key, 40)
    n_pages = PAST_LEN // PAGE_SIZE
    n_used = DEC_BATCH * n_pages
    n_pool = n_used + -(-n_used // 4)

    def nrm(k, shape, scale=1.0):
        return jax.random.normal(k, shape, jnp.float32) * scale

    def gain(k, shape):
        return 1.0 + 0.01 * jax.random.normal(k, shape, jnp.float32)

    page_table = jax.random.permutation(ks[6], n_pool)[:n_used].reshape(DEC_BATCH, n_pages).astype(jnp.int32)
    return {
        'x_prompt': nrm(ks[0], (BATCH, SEQ, D_MODEL)),
        'x_sample': nrm(ks[1], (DEC_BATCH, DEC_SEQ, D_MODEL)),
        'cache_moba_k': nrm(ks[2], (DEPTH, n_pool, PAGE_SIZE, MOBA_HEADS, HEAD_DIM)),
        'cache_moba_v': nrm(ks[3], (DEPTH, n_pool, PAGE_SIZE, MOBA_HEADS, HEAD_DIM)),
        'cache_diff_k': nrm(ks[4], (DEPTH, n_pool, PAGE_SIZE, DIFF_HEADS, 2, HEAD_DIM)),
        'cache_diff_v': nrm(ks[5], (DEPTH, n_pool, PAGE_SIZE, DIFF_HEADS, DIFF_VDIM)),
        'page_table': page_table,
        'p_prompt': nrm(ks[7], (DEPTH, BATCH, SEQ, PLE_DIM)),
        'p_sample': nrm(ks[8], (DEPTH, DEC_BATCH, DEC_SEQ, PLE_DIM)),
        'rel_bias': nrm(ks[9], (REL_BUCKETS, N_BIAS_HEADS), 0.5),
        'g_attn': gain(ks[10], (DEPTH, D_MODEL)),
        'w_in': nrm(ks[11], (DEPTH, D_MODEL, IN_WIDTH), D_MODEL ** -0.5),
        'g_q_moba': gain(ks[12], (DEPTH, HEAD_DIM)),
        'g_k_moba': gain(ks[13], (DEPTH, HEAD_DIM)),
        'g_q_diff': gain(ks[14], (DEPTH, HEAD_DIM)),
        'g_k_diff': gain(ks[15], (DEPTH, HEAD_DIM)),
        'lam_q1': nrm(ks[16], (DEPTH, HEAD_DIM), 0.1),
        'lam_k1': nrm(ks[17], (DEPTH, HEAD_DIM), 0.1),
        'lam_q2': nrm(ks[18], (DEPTH, HEAD_DIM), 0.1),
        'lam_k2': nrm(ks[19], (DEPTH, HEAD_DIM), 0.1),
        'g_subln': gain(ks[20], (DEPTH, DIFF_VDIM)),
        'w_out': nrm(ks[21], (DEPTH, MIX_WIDTH, D_MODEL), MIX_WIDTH ** -0.5),
        'g_ffn': gain(ks[22], (DEPTH, D_MODEL)),
        'w_peer_q': nrm(ks[23], (DEPTH, D_MODEL, PEER_HEADS * PEER_KEY_DIM), D_MODEL ** -0.5),
        'peer_sub_keys': nrm(ks[24], (DEPTH, PEER_HEADS, 2, PEER_NKEYS, PEER_HALF), PEER_HALF ** -0.5),
        'peer_u': nrm(ks[25], (DEPTH, PEER_EXPERTS, D_MODEL), D_MODEL ** -0.5),
        'peer_v': nrm(ks[26], (DEPTH, PEER_EXPERTS, D_MODEL), (PEER_HEADS * PEER_TOPK) ** -0.5),
        'g_ple': gain(ks[27], (DEPTH, D_MODEL)),
        'w_ple_gate': nrm(ks[28], (DEPTH, D_MODEL, D_MODEL), D_MODEL ** -0.5),
        'w_ple_proj': nrm(ks[29], (DEPTH, PLE_DIM, D_MODEL), PLE_DIM ** -0.5),
    }


def reference(x_prompt, x_sample, cache_moba_k, cache_moba_v, cache_diff_k, cache_diff_v, page_table,
              p_prompt, p_sample, rel_bias, g_attn, w_in, g_q_moba, g_k_moba, g_q_diff, g_k_diff,
              lam_q1, lam_k1, lam_q2, lam_k2, g_subln, w_out, g_ffn, w_peer_q, peer_sub_keys,
              peer_u, peer_v, g_ple, w_ple_gate, w_ple_proj):
    bias_m = rel_bias[:, :MOBA_HEADS]
    bias_d = rel_bias[:, MOBA_HEADS:]
    hp, hs = x_prompt, x_sample
    mk_p, mv_p, dk_p, dv_p = [], [], [], []
    mk_s, mv_s, dk_s, dv_s = [], [], [], []
    for i in range(DEPTH):
        lam_init = 0.8 - 0.6 * math.exp(-0.3 * i)
        lam = (jnp.exp(jnp.sum(lam_q1[i].astype(jnp.float32) * lam_k1[i].astype(jnp.float32)))
               - jnp.exp(jnp.sum(lam_q2[i].astype(jnp.float32) * lam_k2[i].astype(jnp.float32)))
               + lam_init)
        qm, km, vm, qd, kd, vd = project_heads(rmsnorm(hp, g_attn[i]), w_in[i],
                                               g_q_moba[i], g_k_moba[i], g_q_diff[i], g_k_diff[i])
        om, od = prompt_mixers(qm, km, vm, qd, kd, vd, lam, bias_m, bias_d)
        hp = hp + merge_heads(om, od, g_subln[i], lam_init, w_out[i])
        hp = channel_and_ple(hp, p_prompt[i], g_ffn[i], w_peer_q[i], peer_sub_keys[i], peer_u[i], peer_v[i],
                             g_ple[i], w_ple_gate[i], w_ple_proj[i])
        mk_p.append(km); mv_p.append(vm); dk_p.append(kd); dv_p.append(vd)
        qm, km, vm, qd, kd, vd = project_heads(rmsnorm(hs, g_attn[i]), w_in[i],
                                               g_q_moba[i], g_k_moba[i], g_q_diff[i], g_k_diff[i])
        om, od = sample_mixers(qm, km, vm, qd, kd, vd, page_table, cache_moba_k, cache_moba_v,
                               cache_diff_k, cache_diff_v, i, lam, bias_m, bias_d)
        hs = hs + merge_heads(om, od, g_subln[i], lam_init, w_out[i])
        hs = channel_and_ple(hs, p_sample[i], g_ffn[i], w_peer_q[i], peer_sub_keys[i], peer_u[i], peer_v[i],
                             g_ple[i], w_ple_gate[i], w_ple_proj[i])
        mk_s.append(km); mv_s.append(vm); dk_s.append(kd); dv_s.append(vd)
    moba_k_prompt = jnp.stack(mk_p)
    moba_v_prompt = jnp.stack(mv_p)
    diff_k_prompt = jnp.stack(dk_p)
    diff_v_prompt = jnp.stack(dv_p)
    moba_k_sample = jnp.stack(mk_s)
    moba_v_sample = jnp.stack(mv_s)
    diff_k_sample = jnp.stack(dk_s)
    diff_v_sample = jnp.stack(dv_s)
    return (hp, hs, moba_k_prompt, moba_v_prompt, diff_k_prompt, diff_v_prompt,
            moba_k_sample, moba_v_sample, diff_k_sample, diff_v_sample)
```

```python
import functools
import math

import numpy as np
import jax
import jax.numpy as jnp
from jax import lax
from jax.experimental import pallas as pl
from jax.experimental.pallas import tpu as pltpu

F32 = jnp.float32
MM = jnp.bfloat16

HEAD_DIM = 64
MOBA_HEADS = 8
MOBA_BLOCK = 256
MOBA_TOPK = 3
DIFF_HEADS = 4
DIFF_VDIM = 2 * HEAD_DIM
MOBA_W = MOBA_HEADS * HEAD_DIM
DIFF_W = DIFF_HEADS * DIFF_VDIM
N_MAPS = MOBA_HEADS + 2 * DIFF_HEADS
ATTN_SCALE = HEAD_DIM ** -0.5
REL_BUCKETS = 32
REL_MAX_EXACT = 16
REL_MAX_DIST = 128
PEER_HEADS = 8
PEER_NKEYS = 128
PEER_HALF = 128
PEER_TOPK = 16
RMS_EPS = 1e-6
LANES = 128
NEG_INF = float("-inf")
VMEM_LIMIT = 56 * 1024 * 1024


def _cparams(*sem):
    return pltpu.CompilerParams(dimension_semantics=sem, vmem_limit_bytes=VMEM_LIMIT)


def _rms(x, g):
    return x * lax.rsqrt(jnp.mean(x * x, axis=-1, keepdims=True) + RMS_EPS) * g


def _dot(a, b):
    return jnp.dot(a.astype(MM), b.astype(MM), preferred_element_type=F32)


def _dot_nt(a, b):
    return lax.dot_general(a.astype(MM), b.astype(MM), (((1,), (1,)), ((), ())), preferred_element_type=F32)


def _dot_tn(a, b):
    return lax.dot_general(a.astype(MM), b.astype(MM), (((0,), (0,)), ((), ())), preferred_element_type=F32)


def _seg_mean(sq, seg):
    hi = sq.astype(jnp.bfloat16)
    lo = (sq - hi.astype(F32)).astype(jnp.bfloat16)
    return (jnp.dot(hi, seg, preferred_element_type=F32) + jnp.dot(lo, seg, preferred_element_type=F32))


def _rel_bucket(rel):
    n = jnp.maximum(rel, 0)
    nf = jnp.maximum(n, 1).astype(F32)
    large = REL_MAX_EXACT + (jnp.log(nf / REL_MAX_EXACT) / math.log(REL_MAX_DIST / REL_MAX_EXACT)
                             * (REL_BUCKETS - REL_MAX_EXACT)).astype(jnp.int32)
    large = jnp.minimum(large, REL_BUCKETS - 1)
    return jnp.where(n < REL_MAX_EXACT, n, large)


def _bias_from_bucket(bucket, bias_smem, col):
    out = jnp.zeros(bucket.shape, F32)
    for k in range(REL_BUCKETS):
        out = jnp.where(bucket == k, bias_smem[k, col], out)
    return out


def _proj_kernel(x_ref, g_ref, w_ref, seg_ref, gq_m, gk_m, gq_d, gk_d,
                 qm_ref, km_ref, vm_ref, qd_ref, kd_ref, vd_ref, *maybe_kmean):
    xn = _rms(x_ref[...], g_ref[...])
    z = _dot(xn, w_ref[...])
    seg = seg_ref[...]

    def headnorm(zc, g):
        return zc * lax.rsqrt(_seg_mean(zc * zc, seg) + RMS_EPS) * g[...]

    w = MOBA_W
    km = headnorm(z[:, w:2 * w], gk_m)
    qm_ref[...] = headnorm(z[:, 0:w], gq_m)
    km_ref[...] = km
    vm_ref[...] = z[:, 2 * w:3 * w]
    qd_ref[...] = headnorm(z[:, 3 * w:4 * w], gq_d)
    kd_ref[...] = headnorm(z[:, 4 * w:5 * w], gk_d)
    vd_ref[...] = z[:, 5 * w:6 * w]
    if maybe_kmean:
        kmean_ref, = maybe_kmean
        for c in range(km.shape[0] // MOBA_BLOCK):
            kmean_ref[0, c:c + 1, :] = jnp.mean(km[c * MOBA_BLOCK:(c + 1) * MOBA_BLOCK], axis=0, keepdims=True)


def _proj(x2d, g_attn, w_in, seg, gq_m, gk_m, gq_d, gk_d, tt, with_kmean):
    t, d = x2d.shape
    nt = t // tt
    row = lambda i: (i, 0)
    const = lambda i: (0, 0)
    vec = pl.BlockSpec((1, MOBA_W), const)
    out_shape = [jax.ShapeDtypeStruct((t, MOBA_W), F32)] * 6
    out_specs = [pl.BlockSpec((tt, MOBA_W), row)] * 6
    if with_kmean:
        nb = tt // MOBA_BLOCK
        out_shape.append(jax.ShapeDtypeStruct((nt, nb, MOBA_W), F32))
        out_specs.append(pl.BlockSpec((1, nb, MOBA_W), lambda i: (i, 0, 0)))
    return pl.pallas_call(
        _proj_kernel,
        grid=(nt,),
        in_specs=[pl.BlockSpec((tt, d), row), pl.BlockSpec((1, d), const), pl.BlockSpec(w_in.shape, const),
                  pl.BlockSpec(seg.shape, const), vec, vec, vec, vec],
        out_specs=out_specs, out_shape=out_shape,
        compiler_params=_cparams("parallel"), name="proj",
    )(x2d, g_attn, w_in, seg, gq_m, gk_m, gq_d, gk_d)


def _top_select_lanes(gate, col, n_sel, n_ok):
    sel = jnp.zeros(gate.shape, F32)
    big = gate.shape[1]
    for k in range(n_sel):
        m = jnp.max(gate, axis=-1, keepdims=True)
        idx = jnp.min(jnp.where(gate == m, col, big), axis=-1, keepdims=True)
        hit = col == idx
        sel = jnp.where(hit & (n_ok > k), 1.0, sel)
        gate = jnp.where(hit, NEG_INF, gate)
    return sel


def _attn_kernel(qi_ref, kj_ref, last_ref, bias_smem, lam_smem,
                 qm_ref, qd_ref, km_ref, kd_ref, vm_ref, vd_ref, kmean_ref,
                 om_ref, od_ref,
                 bias_scr, q_scr, sel_scr, m_scr, l_scr, accm_scr, accd_scr):
    b = pl.program_id(0)
    p = pl.program_id(1)
    qi = qi_ref[p]
    kj = kj_ref[p]
    d = qi - kj
    tq = qm_ref.shape[1]
    tk = km_ref.shape[1]

    @pl.when((b == 0) & (p == 0))
    def _build_bias():
        r = lax.broadcasted_iota(jnp.int32, (tq, tk), 0)
        c = lax.broadcasted_iota(jnp.int32, (tq, tk), 1)
        for dd in range(2):
            rel = dd * tq + r - c
            bucket = _rel_bucket(rel)
            for m in range(N_MAPS):
                col = m if m < MOBA_HEADS else MOBA_HEADS + (m - MOBA_HEADS) // 2
                bt = _bias_from_bucket(bucket, bias_smem, col)
                if dd == 0:
                    bt = jnp.where(rel >= 0, bt, NEG_INF)
                bias_scr[m, dd] = bt
        for m in range(N_MAPS):
            col = m if m < MOBA_HEADS else MOBA_HEADS + (m - MOBA_HEADS) // 2
            bias_scr[m, 2] = jnp.full((tq, tk), bias_smem[REL_BUCKETS - 1, col], F32)

    @pl.when(d == 0)
    def _new_q_tile():
        qm = qm_ref[0]
        qd = qd_ref[0]
        kmean = kmean_ref[0]
        col = lax.broadcasted_iota(jnp.int32, (tq, LANES), 1)
        qblk = jnp.full((tq, 1), qi, jnp.int32)
        for h in range(MOBA_HEADS):
            sl = slice(h * HEAD_DIM, (h + 1) * HEAD_DIM)
            gate = _dot_nt(qm[:, sl], kmean[:, sl])
            gate = jnp.where(col < qblk, gate, NEG_INF)
            sel = _top_select_lanes(gate, col, MOBA_TOPK, qblk)
            sel = jnp.where(col == qblk, 1.0, sel)
            sel_scr[h] = sel.astype(sel_scr.dtype)
            q_scr[h] = (qm[:, sl] * ATTN_SCALE).astype(q_scr.dtype)
        for m in range(2 * DIFF_HEADS):
            sl = slice(m * HEAD_DIM, (m + 1) * HEAD_DIM)
            q_scr[MOBA_HEADS + m] = (qd[:, sl] * ATTN_SCALE).astype(q_scr.dtype)
        m_scr[...] = jnp.full(m_scr.shape, NEG_INF, F32)
        l_scr[...] = jnp.zeros(l_scr.shape, F32)
        accm_scr[...] = jnp.zeros(accm_scr.shape, F32)
        accd_scr[...] = jnp.zeros(accd_scr.shape, F32)

    dsel = jnp.minimum(d, 2)
    onehot_kj = (lax.broadcasted_iota(jnp.int32, (LANES, tk), 0) == kj).astype(sel_scr.dtype)
    km = km_ref[0]
    kd = kd_ref[0]
    vm = vm_ref[0]
    vd = vd_ref[0]

    for m in range(N_MAPS):
        if m < MOBA_HEADS:
            sl = slice(m * HEAD_DIM, (m + 1) * HEAD_DIM)
            k, v = km[:, sl], vm[:, sl]
        else:
            j = m - MOBA_HEADS
            sl = slice(j * HEAD_DIM, (j + 1) * HEAD_DIM)
            hv = j // 2
            k, v = kd[:, sl], vd[:, hv * DIFF_VDIM:(hv + 1) * DIFF_VDIM]
        s = _dot_nt(q_scr[m], k) + bias_scr[m, dsel]
        if m < MOBA_HEADS:
            allowed = jnp.dot(sel_scr[m], onehot_kj, preferred_element_type=F32)
            s = jnp.where(allowed > 0.5, s, NEG_INF)
        m_prev = m_scr[m]
        m_new = jnp.maximum(m_prev, jnp.max(s, axis=-1, keepdims=True))
        alpha = jnp.exp(m_prev - m_new)
        pexp = jnp.exp(s - m_new)
        l_scr[m] = alpha * l_scr[m] + jnp.sum(pexp, axis=-1, keepdims=True)
        m_scr[m] = m_new
        pv = _dot(pexp, v)
        if m < MOBA_HEADS:
            accm_scr[m] = alpha * accm_scr[m] + pv
        else:
            accd_scr[m - MOBA_HEADS] = alpha * accd_scr[m - MOBA_HEADS] + pv

    @pl.when(last_ref[p] == 1)
    def _finish():
        lam = lam_smem[0]
        om_ref[0] = jnp.concatenate([accm_scr[h] / l_scr[h] for h in range(MOBA_HEADS)], axis=-1)
        outs = []
        for h in range(DIFF_HEADS):
            o0 = accd_scr[2 * h] / l_scr[MOBA_HEADS + 2 * h]
            o1 = accd_scr[2 * h + 1] / l_scr[MOBA_HEADS + 2 * h + 1]
            outs.append(o0 - lam * o1)
        od_ref[0] = jnp.concatenate(outs, axis=-1)


def _attn_schedule(nq):
    qi, kj, last = [], [], []
    for i in range(nq):
        order = [i] + list(range(i))
        for n, j in enumerate(order):
            qi.append(i)
            kj.append(j)
            last.append(1 if n == len(order) - 1 else 0)
    return (jnp.asarray(qi, jnp.int32), jnp.asarray(kj, jnp.int32), jnp.asarray(last, jnp.int32))


def _prompt_attn(qm, km, vm, qd, kd, vd, kmean, rel_bias, lam):
    bsz, s, _ = qm.shape
    tq = tk = MOBA_BLOCK
    nq = s // tq
    qi, kj, last = _attn_schedule(nq)
    qspec = pl.BlockSpec((1, tq, MOBA_W), lambda b, p, qi, kj, last: (b, qi[p], 0))
    kspec = pl.BlockSpec((1, tk, MOBA_W), lambda b, p, qi, kj, last: (b, kj[p], 0))
    smem = pl.BlockSpec(memory_space=pltpu.SMEM)
    grid_spec = pltpu.PrefetchScalarGridSpec(
        num_scalar_prefetch=3,
        grid=(bsz, int(qi.shape[0])),
        in_specs=[smem, smem, qspec, qspec, kspec, kspec, kspec, kspec,
                  pl.BlockSpec((1, LANES, MOBA_W), lambda b, p, qi, kj, last: (b, 0, 0))],
        out_specs=[qspec, qspec],
        scratch_shapes=[
            pltpu.VMEM((N_MAPS, 3, tq, tk), F32),
            pltpu.VMEM((N_MAPS, tq, HEAD_DIM), MM),
            pltpu.VMEM((MOBA_HEADS, tq, LANES), MM),
            pltpu.VMEM((N_MAPS, tq, 1), F32),
            pltpu.VMEM((N_MAPS, tq, 1), F32),
            pltpu.VMEM((MOBA_HEADS, tq, HEAD_DIM), F32),
            pltpu.VMEM((2 * DIFF_HEADS, tq, DIFF_VDIM), F32),
        ])
    return pl.pallas_call(
        _attn_kernel, grid_spec=grid_spec,
        out_shape=[jax.ShapeDtypeStruct((bsz, s, MOBA_W), F32), jax.ShapeDtypeStruct((bsz, s, DIFF_W), F32)],
        compiler_params=_cparams("arbitrary", "arbitrary"), name="attn",
    )(qi, kj, last, rel_bias, lam, qm, qd, km, kd, vm, vd, kmean)


PAGES_PER_STEP = 8
PAGE_ROWS = 128
NEW_ROWS = 128
MAPS_PER_TOKEN = 8


def _kmean_pages_kernel(pt_ref, *refs):
    pages = refs[:PAGES_PER_STEP]
    out_ref = refs[PAGES_PER_STEP]
    page = pages[0].shape[2]
    per_block = MOBA_BLOCK // page
    for c in range(PAGES_PER_STEP // per_block):
        acc = jnp.zeros((1, MOBA_W), F32)
        for g in range(per_block):
            acc = acc + jnp.sum(pages[c * per_block + g][0, 0], axis=0, keepdims=True)
        out_ref[0, 0, c:c + 1, :] = acc * (1.0 / MOBA_BLOCK)


def _page_specs(layer):
    def index_map(b, j, pt, g):
        return (layer, pt[b, j * PAGES_PER_STEP + g], 0, 0)
    return [pl.BlockSpec((1, 1, PAGE_ROWS, MOBA_W), functools.partial(index_map, g=g)) for g in range(PAGES_PER_STEP)]


def _kmean_pages(cache_k, page_table, layer):
    nb, n_pages = page_table.shape
    n_steps = n_pages // PAGES_PER_STEP
    bps = PAGES_PER_STEP * PAGE_ROWS // MOBA_BLOCK
    grid_spec = pltpu.PrefetchScalarGridSpec(
        num_scalar_prefetch=1, grid=(nb, n_steps), in_specs=_page_specs(layer),
        out_specs=pl.BlockSpec((1, 1, bps, MOBA_W), lambda b, j, pt: (b, j, 0, 0)))
    return pl.pallas_call(
        _kmean_pages_kernel, grid_spec=grid_spec,
        out_shape=jax.ShapeDtypeStruct((nb, n_steps, bps, MOBA_W), F32),
        compiler_params=_cparams("arbitrary", "arbitrary"), name="kmean_pages",
    )(page_table, *([cache_k] * PAGES_PER_STEP))


def _sattn_kernel(pt_ref, bias_smem, lam_smem,
                  qm_ref, qd_ref, kmn_ref, vmn_ref, kdn_ref, vdn_ref, kmean_ref, *refs):
    n = PAGES_PER_STEP
    mk_pages, mv_pages = refs[0:n], refs[n:2 * n]
    dk_pages, dv_pages = refs[2 * n:3 * n], refs[3 * n:4 * n]
    om_ref, od_ref = refs[4 * n], refs[4 * n + 1]
    qbm_scr, qbd_scr, sel_scr, m_scr, l_scr, accm_scr, accd_scr = refs[4 * n + 2:]
    j = pl.program_id(1)
    n_steps = pl.num_programs(1)
    rows = qm_ref.shape[1]
    tn = rows // MAPS_PER_TOKEN
    tk = n * PAGE_ROWS
    past_len = n_steps * tk
    bps = tk // MOBA_BLOCK

    row_t = lax.broadcasted_iota(jnp.int32, (rows, 1), 0) // MAPS_PER_TOKEN
    row_m = lax.broadcasted_iota(jnp.int32, (rows, 1), 0) % MAPS_PER_TOKEN

    def keep_own_segment(q):
        lane_seg = lax.broadcasted_iota(jnp.int32, q.shape, 1) // HEAD_DIM
        return jnp.where(lane_seg == row_m, q, 0.0)

    def bias_rows(rel, cols):
        bucket = _rel_bucket(rel)
        out = jnp.zeros(rel.shape, F32)
        for k in range(REL_BUCKETS):
            tab = jnp.zeros((rows, 1), F32)
            for mm, cc in enumerate(cols):
                tab = jnp.where(row_m == mm, bias_smem[k, cc], tab)
            out = jnp.where(bucket == k, tab, out)
        return out

    moba_cols = list(range(MOBA_HEADS))
    diff_cols = [MOBA_HEADS + mm // 2 for mm in range(2 * DIFF_HEADS)]

    @pl.when(j == 0)
    def _start():
        qbm = keep_own_segment(qm_ref[0])
        qbm_scr[...] = (qbm * ATTN_SCALE).astype(qbm_scr.dtype)
        qbd_scr[...] = (keep_own_segment(qd_ref[0]) * ATTN_SCALE).astype(qbd_scr.dtype)
        gate = _dot_nt(qbm, kmean_ref[0])
        col = lax.broadcasted_iota(jnp.int32, (rows, LANES), 1)
        n_past = past_len // MOBA_BLOCK
        gate = jnp.where(col < n_past, gate, NEG_INF)
        sel = _top_select_lanes(gate, col, min(MOBA_TOPK, n_past), jnp.full((rows, 1), n_past, jnp.int32))
        sel_scr[...] = sel.astype(sel_scr.dtype)
        m_scr[...] = jnp.full(m_scr.shape, NEG_INF, F32)
        l_scr[...] = jnp.zeros(l_scr.shape, F32)
        accm_scr[...] = jnp.zeros(accm_scr.shape, F32)
        accd_scr[...] = jnp.zeros(accd_scr.shape, F32)

    def update(branch, s, v):
        m_prev = m_scr[branch]
        m_new = jnp.maximum(m_prev, jnp.max(s, axis=-1, keepdims=True))
        m_safe = jnp.where(m_new == NEG_INF, 0.0, m_new)
        alpha = jnp.exp(m_prev - m_safe)
        pexp = jnp.exp(s - m_safe)
        l_scr[branch] = alpha * l_scr[branch] + jnp.sum(pexp, axis=-1, keepdims=True)
        m_scr[branch] = m_new
        acc = accm_scr if branch == 0 else accd_scr
        acc[...] = alpha * acc[...] + _dot(pexp, v)

    kpos = j * tk + lax.broadcasted_iota(jnp.int32, (rows, tk), 1)
    rel = past_len + row_t - kpos

    km = jnp.concatenate([r[0, 0] for r in mk_pages], axis=0)
    vm = jnp.concatenate([r[0, 0] for r in mv_pages], axis=0)
    kd = jnp.concatenate([r[0, 0] for r in dk_pages], axis=0)
    vd = jnp.concatenate([r[0, 0] for r in dv_pages], axis=0)

    blk_of_col = (lax.broadcasted_iota(jnp.int32, (LANES, tk), 1) // MOBA_BLOCK) + j * bps
    onehot = (lax.broadcasted_iota(jnp.int32, (LANES, tk), 0) == blk_of_col).astype(sel_scr.dtype)
    allowed = jnp.dot(sel_scr[...], onehot, preferred_element_type=F32)
    s_m = _dot_nt(qbm_scr[...], km) + bias_rows(rel, moba_cols)
    s_m = jnp.where(allowed > 0.5, s_m, NEG_INF)
    update(0, s_m, vm)
    s_d = _dot_nt(qbd_scr[...], kd) + bias_rows(rel, diff_cols)
    update(1, s_d, vd)

    @pl.when(j == n_steps - 1)
    def _finish():
        cpos = lax.broadcasted_iota(jnp.int32, (rows, NEW_ROWS), 1)
        reln = row_t - cpos
        s_mn = _dot_nt(qbm_scr[...], kmn_ref[0]) + bias_rows(reln, moba_cols)
        update(0, jnp.where(reln >= 0, s_mn, NEG_INF), vmn_ref[0])
        s_dn = _dot_nt(qbd_scr[...], kdn_ref[0]) + bias_rows(reln, diff_cols)
        update(1, jnp.where(reln >= 0, s_dn, NEG_INF), vdn_ref[0])
        lam = lam_smem[0]
        om = accm_scr[...] / l_scr[0]
        lane_h = lax.broadcasted_iota(jnp.int32, (rows, MOBA_W), 1) // HEAD_DIM
        om = jnp.where(lane_h == row_m, om, 0.0)
        od = accd_scr[...] / l_scr[1]
        lane_hd = lax.broadcasted_iota(jnp.int32, (rows, DIFF_W), 1) // DIFF_VDIM
        sign = jnp.where(row_m % 2 == 0, 1.0, -lam)
        od = jnp.where(lane_hd == row_m // 2, od * sign, 0.0)
        for t in range(tn):
            rs = slice(t * MAPS_PER_TOKEN, (t + 1) * MAPS_PER_TOKEN)
            om_ref[0, t:t + 1, :] = jnp.sum(om[rs], axis=0, keepdims=True)
            od_ref[0, t:t + 1, :] = jnp.sum(od[rs], axis=0, keepdims=True)


def _sample_attn(qm, km, vm, qd, kd, vd, kmean, caches, page_table, layer, rel_bias, lam):
    nb, rows, _ = qm.shape
    tn = rows // MAPS_PER_TOKEN
    n_pages = page_table.shape[1]
    n_steps = n_pages // PAGES_PER_STEP
    qspec = pl.BlockSpec((1, rows, MOBA_W), lambda b, j, pt: (b, 0, 0))
    new = pl.BlockSpec((1, NEW_ROWS, MOBA_W), lambda b, j, pt: (b, 0, 0))
    out = pl.BlockSpec((1, tn, MOBA_W), lambda b, j, pt: (b, 0, 0))
    smem = pl.BlockSpec(memory_space=pltpu.SMEM)
    grid_spec = pltpu.PrefetchScalarGridSpec(
        num_scalar_prefetch=1, grid=(nb, n_steps),
        in_specs=[smem, smem, qspec, qspec, new, new, new, new,
                  pl.BlockSpec((1, LANES, MOBA_W), lambda b, j, pt: (b, 0, 0))] + _page_specs(layer) * 4,
        out_specs=[out, out],
        scratch_shapes=[
            pltpu.VMEM((rows, MOBA_W), MM), pltpu.VMEM((rows, MOBA_W), MM),
            pltpu.VMEM((rows, LANES), MM),
            pltpu.VMEM((2, rows, 1), F32), pltpu.VMEM((2, rows, 1), F32),
            pltpu.VMEM((rows, MOBA_W), F32), pltpu.VMEM((rows, DIFF_W), F32),
        ])
    args = []
    for c in caches:
        args += [c] * PAGES_PER_STEP
    return pl.pallas_call(
        _sattn_kernel, grid_spec=grid_spec,
        out_shape=[jax.ShapeDtypeStruct((nb, tn, MOBA_W), F32), jax.ShapeDtypeStruct((nb, tn, DIFF_W), F32)],
        compiler_params=_cparams("arbitrary", "arbitrary"), name="sattn",
    )(page_table, rel_bias, lam, qm, qd, km, vm, kd, vd, kmean, *args)


def _merge_kernel(sub_scale, h_ref, om_ref, od_ref, gsub_ref, wout_ref, gffn_ref, wpq_ref, h1_ref, xn_ref, q_ref):
    od = od_ref[...]
    parts = [om_ref[...]]
    for h in range(DIFF_HEADS):
        parts.append(_rms(od[:, h * DIFF_VDIM:(h + 1) * DIFF_VDIM], gsub_ref[...]) * sub_scale)
    y = jnp.concatenate(parts, axis=-1)
    h1 = h_ref[...] + _dot(y, wout_ref[...])
    h1_ref[...] = h1
    xn = _rms(h1, gffn_ref[...])
    xn_ref[...] = xn.astype(xn_ref.dtype)
    q_ref[...] = _dot(xn, wpq_ref[...]).astype(q_ref.dtype)


def _merge(h, om, od, g_sub, w_out, g_ffn, w_pq, sub_scale, tt):
    t, d = h.shape
    nq = w_pq.shape[1]
    row = lambda i: (i, 0)
    const = lambda i: (0, 0)
    return pl.pallas_call(
        functools.partial(_merge_kernel, sub_scale),
        grid=(t // tt,),
        in_specs=[pl.BlockSpec((tt, d), row), pl.BlockSpec((tt, MOBA_W), row), pl.BlockSpec((tt, DIFF_W), row),
                  pl.BlockSpec((1, DIFF_VDIM), const), pl.BlockSpec(w_out.shape, const),
                  pl.BlockSpec((1, d), const), pl.BlockSpec(w_pq.shape, const)],
        out_specs=[pl.BlockSpec((tt, d), row), pl.BlockSpec((tt, d), row), pl.BlockSpec((tt, nq), row)],
        out_shape=[jax.ShapeDtypeStruct((t, d), F32), jax.ShapeDtypeStruct((t, d), MM),
                   jax.ShapeDtypeStruct((t, nq), MM)],
        compiler_params=_cparams("parallel"), name="merge",
    )(h, om, od, g_sub, w_out, g_ffn, w_pq)


def _topk_rows(s, k):
    n = s.shape[0]
    row = lax.broadcasted_iota(jnp.int32, s.shape, 0)
    rank = jnp.full(s.shape, k, jnp.int32)
    vals = jnp.zeros((k, s.shape[1]), F32)
    vrow = lax.broadcasted_iota(jnp.int32, vals.shape, 0)
    for r in range(k):
        m = jnp.max(s, axis=0, keepdims=True)
        idx = jnp.min(jnp.where(s == m, row, n), axis=0, keepdims=True)
        hit = row == idx
        rank = jnp.where(hit, r, rank)
        s = jnp.where(hit, NEG_INF, s)
        vals = jnp.where(vrow == r, m, vals)
    return vals, rank


_CAND_FULL = 8


def _peersel_kernel(q_ref, keys_ref, rank2_ref, cnt_ref, e1_ref, e2_ref):
    k = PEER_TOPK
    q = q_ref[...]
    s1 = _dot_nt(keys_ref[0, 0], q[:, :PEER_HALF])
    s2 = _dot_nt(keys_ref[0, 1], q[:, PEER_HALF:])
    v1, rank1 = _topk_rows(s1, k)
    v2, rank2 = _topk_rows(s2, k)
    slabs = [v1[r:r + 1] + v2 for r in range(_CAND_FULL)]
    slabs.append(v1[_CAND_FULL:] + v2[0:1])
    cand = jnp.concatenate(slabs, axis=0)
    picked, crank = _topk_rows(cand, k)
    taken = (crank < k).astype(F32)
    z = jnp.sum(jnp.exp(picked - picked[0:1]), axis=0, keepdims=True)
    cnt = jnp.zeros(s1.shape, F32)
    for r in range(k):
        if r < _CAND_FULL:
            c_r = jnp.sum(taken[r * k:(r + 1) * k], axis=0, keepdims=True)
        else:
            c_r = taken[_CAND_FULL * k + r - _CAND_FULL:_CAND_FULL * k + r - _CAND_FULL + 1]
        cnt = jnp.where(rank1 == r, c_r, cnt)
    rank2_ref[0] = rank2.astype(rank2_ref.dtype)
    cnt_ref[0] = cnt.astype(cnt_ref.dtype)
    e1_ref[0] = jnp.exp(s1 - v1[0:1]) / z
    e2_ref[0] = jnp.exp(s2 - v2[0:1])


def _peersel(q, sub_keys, tt):
    t = q.shape[0]
    spec = pl.BlockSpec((1, PEER_NKEYS, tt), lambda i, h: (h, 0, i))
    shp = jax.ShapeDtypeStruct((PEER_HEADS, PEER_NKEYS, t), F32)
    return pl.pallas_call(
        _peersel_kernel,
        grid=(t // tt, PEER_HEADS),
        in_specs=[pl.BlockSpec((tt, 2 * PEER_HALF), lambda i, h: (i, h)),
                  pl.BlockSpec((1, 2, PEER_NKEYS, PEER_HALF), lambda i, h: (h, 0, 0, 0))],
        out_specs=[spec] * 4, out_shape=[shp] * 4,
        compiler_params=_cparams("parallel", "arbitrary"), name="peersel",
    )(q, sub_keys)


PEER_I1_PER_STEP = 4


def _peerffn_kernel(xn_ref, u_ref, v_ref, rank2_ref, cnt_ref, e1_ref, e2_ref, out_ref, acc_scr):
    j = pl.program_id(1)

    @pl.when(j == 0)
    def _zero():
        acc_scr[...] = jnp.zeros(acc_scr.shape, F32)

    hid = _dot_nt(u_ref[...], xn_ref[...])
    act = 0.5 * hid * (1.0 + lax.erf(hid * (2.0 ** -0.5)))
    blocks = []
    for a in range(PEER_I1_PER_STEP):
        i1 = j * PEER_I1_PER_STEP + a
        w = jnp.zeros((PEER_NKEYS, xn_ref.shape[0]), F32)
        for h in range(PEER_HEADS):
            cnt = cnt_ref[h, pl.ds(i1, 1), :]
            e1 = e1_ref[h, pl.ds(i1, 1), :]
            w = w + jnp.where(rank2_ref[h] < cnt, e2_ref[h], 0.0) * e1
        blocks.append(w * act[a * PEER_NKEYS:(a + 1) * PEER_NKEYS])
    coef = jnp.concatenate(blocks, axis=0)
    acc_scr[...] += _dot_tn(v_ref[...], coef)

    @pl.when(j == pl.num_programs(1) - 1)
    def _store():
        out_ref[...] = acc_scr[...].T


def _peerffn(xn, u, v, rank2, cnt, e1, e2, tt):
    t, d = xn.shape
    ne = PEER_I1_PER_STEP * PEER_NKEYS
    sel = pl.BlockSpec((PEER_HEADS, PEER_NKEYS, tt), lambda i, j: (0, 0, i))
    return pl.pallas_call(
        _peerffn_kernel,
        grid=(t // tt, PEER_NKEYS // PEER_I1_PER_STEP),
        in_specs=[pl.BlockSpec((tt, d), lambda i, j: (i, 0)),
                  pl.BlockSpec((ne, d), lambda i, j: (j, 0)), pl.BlockSpec((ne, d), lambda i, j: (j, 0)),
                  sel, sel, sel, sel],
        out_specs=pl.BlockSpec((tt, d), lambda i, j: (i, 0)),
        out_shape=jax.ShapeDtypeStruct((t, d), F32),
        scratch_shapes=[pltpu.VMEM((d, tt), F32)],
        compiler_params=_cparams("parallel", "arbitrary"), name="peerffn",
    )(xn, u, v, rank2, cnt, e1, e2)


def _ple_kernel(h_ref, peer_ref, p_ref, g_ref, wg_ref, wp_ref, out_ref):
    h2 = h_ref[...] + peer_ref[...]
    gate = jax.nn.sigmoid(_dot(_rms(h2, g_ref[...]), wg_ref[...]))
    out_ref[...] = h2 + gate * _dot(p_ref[...], wp_ref[...])


def _ple(h1, peer, p, g_ple, w_pg, w_pp, tt):
    t, d = h1.shape
    row = lambda i: (i, 0)
    const = lambda i: (0, 0)
    return pl.pallas_call(
        _ple_kernel,
        grid=(t // tt,),
        in_specs=[pl.BlockSpec((tt, d), row), pl.BlockSpec((tt, d), row), pl.BlockSpec((tt, p.shape[1]), row),
                  pl.BlockSpec((1, d), const), pl.BlockSpec(w_pg.shape, const), pl.BlockSpec(w_pp.shape, const)],
        out_specs=pl.BlockSpec((tt, d), row),
        out_shape=jax.ShapeDtypeStruct((t, d), F32),
        compiler_params=_cparams("parallel"), name="ple",
    )(h1, peer, p, g_ple, w_pg, w_pp)


def _pad_rows(a, n):
    return jnp.pad(a, ((0, 0), (0, n - a.shape[1]), (0, 0)))


def _token_tile(t, pref):
    return pref if t % pref == 0 else t


def kernel(x_prompt, x_sample, cache_moba_k, cache_moba_v, cache_diff_k, cache_diff_v, page_table, p_prompt, p_sample, rel_bias, g_attn, w_in, g_q_moba, g_k_moba, g_q_diff, g_k_diff, lam_q1, lam_k1, lam_q2, lam_k2, g_subln, w_out, g_ffn, w_peer_q, peer_sub_keys, peer_u, peer_v, g_ple, w_ple_gate, w_ple_proj):
    bsz, seq, d = x_prompt.shape
    nb, tn, _ = x_sample.shape
    depth = w_in.shape[0]
    n_pool, page = cache_moba_k.shape[1], cache_moba_k.shape[2]
    assert page == PAGE_ROWS and seq % MOBA_BLOCK == 0 and seq // MOBA_BLOCK <= LANES
    past_len = page_table.shape[1] * page
    assert past_len % (PAGES_PER_STEP * page) == 0 and past_len // MOBA_BLOCK + 1 <= LANES

    seg = jnp.asarray(np.kron(np.eye(MOBA_W // HEAD_DIM), np.full((HEAD_DIM, HEAD_DIM), 1.0 / HEAD_DIM)), jnp.bfloat16)
    caches = [c.reshape(depth, n_pool, page, MOBA_W) for c in (cache_moba_k, cache_moba_v, cache_diff_k, cache_diff_v)]
    tile8 = lambda g: jnp.tile(g, MOBA_W // HEAD_DIM)[None, :]

    hp = x_prompt.reshape(bsz * seq, d)
    hs = x_sample.reshape(nb * tn, d)
    new_p = [[] for _ in range(4)]
    new_s = [[] for _ in range(4)]
    for i in range(depth):
        lam_init = 0.8 - 0.6 * math.exp(-0.3 * i)
        lam = (jnp.exp(jnp.sum(lam_q1[i] * lam_k1[i])) - jnp.exp(jnp.sum(lam_q2[i] * lam_k2[i])) + lam_init)
        lam = lam.reshape(1).astype(F32)
        w_in_i = w_in[i].astype(MM)
        w_out_i = w_out[i].astype(MM)
        w_pq_i = w_peer_q[i].astype(MM)
        u_i = peer_u[i].astype(MM)
        v_i = peer_v[i].astype(MM)
        w_pg_i = w_ple_gate[i].astype(MM)
        w_pp_i = w_ple_proj[i].astype(MM)
        gains = (tile8(g_q_moba[i]), tile8(g_k_moba[i]), tile8(g_q_diff[i]), tile8(g_k_diff[i]))

        tp = _token_tile(bsz * seq, 512)
        qm, km, vm, qd, kd, vd, kmean = _proj(hp, g_attn[i][None], w_in_i, seg, *gains, tt=tp, with_kmean=True)
        for lst, a in zip(new_p, (km, vm, kd, vd)):
            lst.append(a)
        kmean = _pad_rows(kmean.reshape(bsz, seq // MOBA_BLOCK, MOBA_W), LANES)
        r3 = lambda a: a.reshape(bsz, seq, MOBA_W)
        om_p, od_p = _prompt_attn(r3(qm), r3(km), r3(vm), r3(qd), r3(kd), r3(vd), kmean, rel_bias, lam)

        ts = _token_tile(nb * tn, 128)
        qm, km, vm, qd, kd, vd = _proj(hs, g_attn[i][None], w_in_i, seg, *gains, tt=ts, with_kmean=False)
        for lst, a in zip(new_s, (km, vm, kd, vd)):
            lst.append(a)
        kmean_s = _kmean_pages(caches[0], page_table, i).reshape(nb, past_len // MOBA_BLOCK, MOBA_W)
        kmean_s = _pad_rows(kmean_s, LANES)
        rep = lambda a: jnp.repeat(a.reshape(nb, tn, MOBA_W), MAPS_PER_TOKEN, axis=1)
        pad = lambda a: _pad_rows(a.reshape(nb, tn, MOBA_W), NEW_ROWS)
        om_s, od_s = _sample_attn(rep(qm), pad(km), pad(vm), rep(qd), pad(kd), pad(vd), kmean_s, caches, page_table,
                                  i, rel_bias, lam)

        outs = []
        for h, om, od, p, tt in ((hp, om_p.reshape(-1, MOBA_W), od_p.reshape(-1, DIFF_W),
                                  p_prompt[i].reshape(bsz * seq, -1), tp),
                                 (hs, om_s.reshape(-1, MOBA_W), od_s.reshape(-1, DIFF_W),
                                  p_sample[i].reshape(nb * tn, -1), ts)):
            h1, xn, q = _merge(h, om, od, g_subln[i][None], w_out_i, g_ffn[i][None], w_pq_i, 1.0 - lam_init, tt)
            rank2, cnt, e1, e2 = _peersel(q, peer_sub_keys[i], tt)
            peer = _peerffn(xn, u_i, v_i, rank2, cnt, e1, e2, tt)
            outs.append(_ple(h1, peer, p, g_ple[i][None], w_pg_i, w_pp_i, tt))
        hp, hs = outs

    def stack(lst, lead, tail):
        return jnp.stack([a.reshape(*lead, *tail) for a in lst])

    lp, ls = (bsz, seq), (nb, tn)
    tails = ((MOBA_HEADS, HEAD_DIM), (MOBA_HEADS, HEAD_DIM), (DIFF_HEADS, 2, HEAD_DIM), (DIFF_HEADS, DIFF_VDIM))
    return (hp.reshape(bsz, seq, d), hs.reshape(nb, tn, d),
            *[stack(l, lp, t) for l, t in zip(new_p, tails)],
            *[stack(l, ls, t) for l, t in zip(new_s, tails)])
```

```python
import functools
import math

import numpy as np
import jax
import jax.numpy as jnp
from jax import lax
from jax.experimental import pallas as pl
from jax.experimental.pallas import tpu as pltpu

F32 = jnp.float32
MM = jnp.bfloat16

HEAD_DIM = 64
MOBA_HEADS = 8
MOBA_BLOCK = 256
MOBA_TOPK = 3
DIFF_HEADS = 4
DIFF_VDIM = 2 * HEAD_DIM
MOBA_W = MOBA_HEADS * HEAD_DIM
DIFF_W = DIFF_HEADS * DIFF_VDIM
QK_W = MOBA_W + DIFF_W
N_MAPS = MOBA_HEADS + 2 * DIFF_HEADS
ATTN_SCALE = HEAD_DIM ** -0.5
LOG2E = math.log2(math.e)
REL_BUCKETS = 32
REL_MAX_EXACT = 16
REL_MAX_DIST = 128
PEER_HEADS = 8
PEER_NKEYS = 128
PEER_HALF = 128
PEER_TOPK = 16
RMS_EPS = 1e-6
LANES = 128
SUBLANES = 8
NEG_INF = float("-inf")
VMEM_LIMIT = 56 * 1024 * 1024


def _cparams(*sem):
    return pltpu.CompilerParams(dimension_semantics=sem, vmem_limit_bytes=VMEM_LIMIT)


def _rms(x, g):
    return x * lax.rsqrt(jnp.mean(x * x, axis=-1, keepdims=True) + RMS_EPS) * g


def _dot(a, b):
    return jnp.dot(a.astype(MM), b.astype(MM), preferred_element_type=F32)


def _dot_nt(a, b):
    return lax.dot_general(a.astype(MM), b.astype(MM), (((1,), (1,)), ((), ())), preferred_element_type=F32)


def _dot_tn(a, b):
    return lax.dot_general(a.astype(MM), b.astype(MM), (((0,), (0,)), ((), ())), preferred_element_type=F32)


def _seg_mean(sq, seg):
    hi = sq.astype(jnp.bfloat16)
    lo = (sq - hi.astype(F32)).astype(jnp.bfloat16)
    return (jnp.dot(hi, seg, preferred_element_type=F32) + jnp.dot(lo, seg, preferred_element_type=F32))


def _rel_bucket(rel):
    n = jnp.maximum(rel, 0)
    nf = jnp.maximum(n, 1).astype(F32)
    large = REL_MAX_EXACT + (jnp.log(nf / REL_MAX_EXACT) / math.log(REL_MAX_DIST / REL_MAX_EXACT)
                             * (REL_BUCKETS - REL_MAX_EXACT)).astype(jnp.int32)
    large = jnp.minimum(large, REL_BUCKETS - 1)
    return jnp.where(n < REL_MAX_EXACT, n, large)


def _bias_col(m):
    return m if m < MOBA_HEADS else MOBA_HEADS + (m - MOBA_HEADS) // 2


def _proj_kernel(x_ref, g_ref, wkv_ref, wqt_ref, wvt_ref, seg_ref, gk_ref, gqt_ref,
                 km_ref, vm_ref, kd_ref, vd_ref, kb_ref, qt_ref, vt_ref, *maybe_kmean):
    xn = _rms(x_ref[...], g_ref[...]).astype(MM)
    z = jnp.dot(xn, wkv_ref[...], preferred_element_type=F32)
    seg = seg_ref[...]
    gk = gk_ref[...]
    ks = []
    for c in range(2):
        zc = z[:, c * MOBA_W:(c + 1) * MOBA_W]
        ks.append(zc * lax.rsqrt(_seg_mean(zc * zc, seg) + RMS_EPS) * gk[:, c * MOBA_W:(c + 1) * MOBA_W])
    km_ref[...] = ks[0]
    kd_ref[...] = ks[1]
    kb_ref[...] = jnp.concatenate(ks, axis=-1).astype(kb_ref.dtype)
    vm_ref[...] = z[:, 2 * MOBA_W:3 * MOBA_W]
    vd_ref[...] = z[:, 3 * MOBA_W:4 * MOBA_W]
    tt = xn.shape[0]
    zq = _dot_nt(wqt_ref[...], xn).reshape(QK_W // HEAD_DIM, HEAD_DIM, tt)
    ms = jnp.mean(zq * zq, axis=1, keepdims=True)
    qt = (zq * lax.rsqrt(ms + RMS_EPS)).reshape(QK_W, tt) * gqt_ref[...]
    qt_ref[...] = qt.astype(qt_ref.dtype)
    vt_ref[...] = _dot_nt(wvt_ref[...], xn).astype(vt_ref.dtype)
    if maybe_kmean:
        kmean_ref, = maybe_kmean
        for c in range(tt // MOBA_BLOCK):
            kmean_ref[0, c:c + 1, :] = jnp.mean(ks[0][c * MOBA_BLOCK:(c + 1) * MOBA_BLOCK], axis=0, keepdims=True)


def _proj(x2d, g_attn, wkv, wqt, wvt, seg, gk, gqt, tt, with_kmean):
    t, d = x2d.shape
    nt = t // tt
    row = lambda i: (i, 0)
    col = lambda i: (0, i)
    const = lambda i: (0, 0)
    full = lambda a: pl.BlockSpec(a.shape, const)
    out_shape = [jax.ShapeDtypeStruct((t, MOBA_W), F32)] * 4 + [
        jax.ShapeDtypeStruct((t, QK_W), MM), jax.ShapeDtypeStruct((QK_W, t), MM), jax.ShapeDtypeStruct((QK_W, t), MM)]
    out_specs = [pl.BlockSpec((tt, MOBA_W), row)] * 4 + [
        pl.BlockSpec((tt, QK_W), row), pl.BlockSpec((QK_W, tt), col), pl.BlockSpec((QK_W, tt), col)]
    if with_kmean:
        nb = tt // MOBA_BLOCK
        out_shape.append(jax.ShapeDtypeStruct((nt, nb, MOBA_W), F32))
        out_specs.append(pl.BlockSpec((1, nb, MOBA_W), lambda i: (i, 0, 0)))
    return pl.pallas_call(
        _proj_kernel,
        grid=(nt,),
        in_specs=[pl.BlockSpec((tt, d), row), full(g_attn), full(wkv), full(wqt), full(wvt), full(seg), full(gk),
                  full(gqt)],
        out_specs=out_specs, out_shape=out_shape,
        compiler_params=_cparams("parallel"), name="proj",
    )(x2d, g_attn, wkv, wqt, wvt, seg, gk, gqt)


def _top_select(gate, pos, axis, n_sel, n_ok):
    sel = jnp.zeros(gate.shape, F32)
    big = gate.shape[axis]
    for k in range(n_sel):
        m = jnp.max(gate, axis=axis, keepdims=True)
        idx = jnp.min(jnp.where(gate == m, pos, big), axis=axis, keepdims=True)
        hit = pos == idx
        sel = jnp.where(hit & (n_ok > k), 1.0, sel)
        gate = jnp.where(hit, NEG_INF, gate)
    return sel


def _attn_kernel(qi_ref, kj_ref, last_ref, bias_smem, lam_smem,
                 qt_ref, kb_ref, vt_ref, kmean_ref,
                 om_ref, od_ref,
                 bias_scr, qp_scr, sel_scr, m_scr, l_scr, accm_scr, accd_scr, s_scr):
    b = pl.program_id(0)
    p = pl.program_id(1)
    qi = qi_ref[p]
    kj = kj_ref[p]
    d = qi - kj
    tq = qt_ref.shape[1]
    tk = kb_ref.shape[0]

    @pl.when((b == 0) & (p == 0))
    def _build_bias():
        c = lax.broadcasted_iota(jnp.int32, (tk, tq), 0)
        r = lax.broadcasted_iota(jnp.int32, (tk, tq), 1)
        for dd in range(2):
            rel = dd * tq + r - c
            bucket = _rel_bucket(rel)
            for m in range(N_MAPS):
                bt = jnp.zeros((tk, tq), F32)
                for k in range(REL_BUCKETS):
                    bt = jnp.where(bucket == k, bias_smem[k, _bias_col(m)], bt)
                if dd == 0:
                    bt = jnp.where(rel >= 0, bt, NEG_INF)
                bias_scr[m, dd] = bt * LOG2E
        for m in range(N_MAPS):
            bias_scr[m, 2] = jnp.full((tk, tq), bias_smem[REL_BUCKETS - 1, _bias_col(m)], F32) * LOG2E

    @pl.when(d == 0)
    def _new_q_tile():
        half = lax.broadcasted_iota(jnp.int32, (2 * HEAD_DIM, tq), 0) // HEAD_DIM
        for j in range(N_MAPS // 2):
            qpair = qt_ref[j * 2 * HEAD_DIM:(j + 1) * 2 * HEAD_DIM, :].astype(F32)
            for e in range(2):
                qp_scr[2 * j + e] = jnp.where(half == e, qpair, 0.0).astype(qp_scr.dtype)
        blk = lax.broadcasted_iota(jnp.int32, (LANES, tq), 0)
        qblk = qi
        kmean = kmean_ref[0]
        for h in range(MOBA_HEADS):
            j = h // 2
            gate = _dot(kmean[:, j * 2 * HEAD_DIM:(j + 1) * 2 * HEAD_DIM], qp_scr[h])
            gate = jnp.where(blk < qblk, gate, NEG_INF)
            sel = _top_select(gate, blk, 0, MOBA_TOPK, qblk)
            sel_scr[h] = jnp.where(blk == qblk, 1.0, sel)
        m_scr[...] = jnp.full(m_scr.shape, NEG_INF, F32)
        l_scr[...] = jnp.zeros(l_scr.shape, F32)
        accm_scr[...] = jnp.zeros(accm_scr.shape, F32)
        accd_scr[...] = jnp.zeros(accd_scr.shape, F32)

    dsel = jnp.minimum(d, 2)

    def scores(m):
        j = m // 2
        k_pair = kb_ref[:, j * 2 * HEAD_DIM:(j + 1) * 2 * HEAD_DIM]
        return jnp.dot(k_pair, qp_scr[m], preferred_element_type=F32)

    n_buf = s_scr.shape[0]
    for m in range(n_buf - 1):
        s_scr[m] = scores(m)
    for m in range(N_MAPS):
        if m + n_buf - 1 < N_MAPS:
            s_scr[(m + n_buf - 1) % n_buf] = scores(m + n_buf - 1)
        s = s_scr[m % n_buf] + bias_scr[m, dsel]
        if m < MOBA_HEADS:
            s = jnp.where(sel_scr[m, pl.ds(kj, 1), :] > 0.5, s, NEG_INF)
            vt = vt_ref[m * HEAD_DIM:(m + 1) * HEAD_DIM, :]
        else:
            hv = (m - MOBA_HEADS) // 2
            vt = vt_ref[MOBA_W + hv * DIFF_VDIM:MOBA_W + (hv + 1) * DIFF_VDIM, :]
        m_prev = m_scr[m]
        m_new = jnp.maximum(m_prev, jnp.max(s, axis=0, keepdims=True))
        alpha = jnp.exp2(m_prev - m_new)
        pexp = jnp.exp2(s - m_new)
        l_scr[m] = alpha * l_scr[m] + jnp.sum(pexp, axis=0, keepdims=True)
        m_scr[m] = m_new
        pv = jnp.dot(vt, pexp.astype(MM), preferred_element_type=F32)
        if m < MOBA_HEADS:
            accm_scr[m] = alpha * accm_scr[m] + pv
        else:
            accd_scr[m - MOBA_HEADS] = alpha * accd_scr[m - MOBA_HEADS] + pv

    @pl.when(last_ref[p] == 1)
    def _finish():
        lam = lam_smem[0]
        omt = jnp.concatenate([accm_scr[h] / l_scr[h] for h in range(MOBA_HEADS)], axis=0)
        om_ref[...] = omt.T
        outs = []
        for h in range(DIFF_HEADS):
            o0 = accd_scr[2 * h] / l_scr[MOBA_HEADS + 2 * h]
            o1 = accd_scr[2 * h + 1] / l_scr[MOBA_HEADS + 2 * h + 1]
            outs.append(o0 - lam * o1)
        od_ref[...] = jnp.concatenate(outs, axis=0).T


SCORE_BUFS = 3


def _attn_schedule(nq):
    qi, kj, last = [], [], []
    for i in range(nq):
        order = [i] + list(range(i))
        for n, j in enumerate(order):
            qi.append(i)
            kj.append(j)
            last.append(1 if n == len(order) - 1 else 0)
    return (jnp.asarray(qi, jnp.int32), jnp.asarray(kj, jnp.int32), jnp.asarray(last, jnp.int32))


def _prompt_attn(qt, kb, vt, kmean, rel_bias, lam, bsz):
    s = kb.shape[0] // bsz
    tq = tk = MOBA_BLOCK
    nq = s // tq
    qi, kj, last = _attn_schedule(nq)
    qcol = pl.BlockSpec((QK_W, tq), lambda b, p, qi, kj, last: (0, b * nq + qi[p]))
    kcol = pl.BlockSpec((QK_W, tk), lambda b, p, qi, kj, last: (0, b * nq + kj[p]))
    krow = pl.BlockSpec((tk, QK_W), lambda b, p, qi, kj, last: (b * nq + kj[p], 0))
    orow = pl.BlockSpec((tq, MOBA_W), lambda b, p, qi, kj, last: (b * nq + qi[p], 0))
    smem = pl.BlockSpec(memory_space=pltpu.SMEM)
    grid_spec = pltpu.PrefetchScalarGridSpec(
        num_scalar_prefetch=3,
        grid=(bsz, int(qi.shape[0])),
        in_specs=[smem, smem, qcol, krow, kcol,
                  pl.BlockSpec((1, LANES, MOBA_W), lambda b, p, qi, kj, last: (b, 0, 0))],
        out_specs=[orow, orow],
        scratch_shapes=[
            pltpu.VMEM((N_MAPS, 3, tk, tq), F32),
            pltpu.VMEM((N_MAPS, 2 * HEAD_DIM, tq), MM),
            pltpu.VMEM((MOBA_HEADS, LANES, tq), F32),
            pltpu.VMEM((N_MAPS, 1, tq), F32),
            pltpu.VMEM((N_MAPS, 1, tq), F32),
            pltpu.VMEM((MOBA_HEADS, HEAD_DIM, tq), F32),
            pltpu.VMEM((2 * DIFF_HEADS, DIFF_VDIM, tq), F32),
            pltpu.VMEM((SCORE_BUFS, tk, tq), F32),
        ])
    t = kb.shape[0]
    return pl.pallas_call(
        _attn_kernel, grid_spec=grid_spec,
        out_shape=[jax.ShapeDtypeStruct((t, MOBA_W), F32), jax.ShapeDtypeStruct((t, DIFF_W), F32)],
        compiler_params=_cparams("arbitrary", "arbitrary"), name="attn",
    )(qi, kj, last, rel_bias, lam, qt, kb, vt, kmean)


PAGES_PER_STEP = 8
PAGE_ROWS = 128
TOKEN_ROWS = SUBLANES
MAPS_PER_TOKEN = 8
SROWS = MAPS_PER_TOKEN * TOKEN_ROWS


def _sattn_kernel(pt_ref, bias_smem, lam_smem,
                  qm_ref, qd_ref, kmn_ref, vmn_ref, kdn_ref, vdn_ref, *refs, tn):
    n = PAGES_PER_STEP
    mk_pages, mv_pages = refs[0:n], refs[n:2 * n]
    dk_pages, dv_pages = refs[2 * n:3 * n], refs[3 * n:4 * n]
    om_ref, od_ref = refs[4 * n], refs[4 * n + 1]
    qbm_scr, qbd_scr, gate_scr, sel_scr, m_scr, l_scr, far_scr, accm_scr, accd_scr = refs[4 * n + 2:]
    b = pl.program_id(0)
    ph = pl.program_id(1)
    j = pl.program_id(2)
    n_steps = pl.num_programs(2)
    tk = n * PAGE_ROWS
    past_len = n_steps * tk
    bps = tk // MOBA_BLOCK
    n_past = past_len // MOBA_BLOCK
    rows = SROWS

    row_m = lax.broadcasted_iota(jnp.int32, (rows, 1), 0) // TOKEN_ROWS
    row_t = lax.broadcasted_iota(jnp.int32, (rows, 1), 0) % TOKEN_ROWS
    moba_cols = [_bias_col(m) for m in range(MOBA_HEADS)]
    diff_cols = [_bias_col(MOBA_HEADS + m) for m in range(2 * DIFF_HEADS)]

    def bias_rows(rel, cols):
        bucket = _rel_bucket(rel)
        out = jnp.zeros(rel.shape, F32)
        for k in range(REL_BUCKETS):
            tab = jnp.zeros((rows, 1), F32)
            for mm, cc in enumerate(cols):
                tab = jnp.where(row_m == mm, bias_smem[k, cc], tab)
            out = jnp.where(bucket == k, tab, out)
        return out * LOG2E

    def page_bias(branch, rel, cols):
        return lax.cond(j == n_steps - 1, lambda: bias_rows(rel, cols),
                        lambda: jnp.broadcast_to(far_scr[branch], rel.shape))

    def kt_step(pages):
        return jnp.concatenate([r[0, 0].reshape(MOBA_W, PAGE_ROWS) for r in pages], axis=-1).astype(MM)

    @pl.when((ph == 0) & (j == 0))
    def _start():
        def keep_own_segment(q):
            lane_seg = lax.broadcasted_iota(jnp.int32, q.shape, 1) // HEAD_DIM
            return jnp.where(lane_seg == row_m, q, 0.0)
        qbm_scr[...] = keep_own_segment(qm_ref[0]).astype(qbm_scr.dtype)
        qbd_scr[...] = keep_own_segment(qd_ref[0]).astype(qbd_scr.dtype)
        gate_scr[...] = jnp.full(gate_scr.shape, NEG_INF, F32)
        far = jnp.full((rows, 1), REL_MAX_DIST, jnp.int32)
        far_scr[0] = bias_rows(far, moba_cols)
        far_scr[1] = bias_rows(far, diff_cols)

    @pl.when(ph == 0)
    def _gate():
        s = jnp.dot(qbm_scr[...], kt_step(mk_pages), preferred_element_type=F32)
        col = lax.broadcasted_iota(jnp.int32, (rows, LANES), 1)
        g = gate_scr[...]
        for c in range(bps):
            bs = jnp.sum(s[:, c * MOBA_BLOCK:(c + 1) * MOBA_BLOCK], axis=-1, keepdims=True)
            g = jnp.where(col == j * bps + c, bs, g)
        gate_scr[...] = g

        @pl.when(j == n_steps - 1)
        def _select():
            sel = _top_select(g, col, 1, min(MOBA_TOPK, n_past), jnp.full((rows, 1), n_past, jnp.int32))
            sel_scr[...] = sel.astype(sel_scr.dtype)
            m_scr[...] = jnp.full(m_scr.shape, NEG_INF, F32)
            l_scr[...] = jnp.zeros(l_scr.shape, F32)
            accm_scr[...] = jnp.zeros(accm_scr.shape, F32)
            accd_scr[...] = jnp.zeros(accd_scr.shape, F32)

    def softmax_step(branch, s):
        m_prev = m_scr[branch]
        m_new = jnp.maximum(m_prev, jnp.max(s, axis=-1, keepdims=True))
        m_safe = jnp.where(m_new == NEG_INF, 0.0, m_new)
        alpha = jnp.exp2(m_prev - m_safe)
        pexp = jnp.exp2(s - m_safe)
        l_scr[branch] = alpha * l_scr[branch] + jnp.sum(pexp, axis=-1, keepdims=True)
        m_scr[branch] = m_new
        return alpha, pexp.astype(MM)

    def diff_pv(alpha, pexp, v_of_head):
        for h in range(DIFF_HEADS):
            rs = slice(h * 2 * TOKEN_ROWS, (h + 1) * 2 * TOKEN_ROWS)
            accd_scr[rs] = alpha[rs] * accd_scr[rs] + jnp.dot(pexp[rs], v_of_head(h), preferred_element_type=F32)

    @pl.when(ph == 1)
    def _attend():
        kpos = j * tk + lax.broadcasted_iota(jnp.int32, (rows, tk), 1)
        rel = past_len + row_t - kpos
        blk_of_col = (lax.broadcasted_iota(jnp.int32, (LANES, tk), 1) // MOBA_BLOCK) + j * bps
        onehot = (lax.broadcasted_iota(jnp.int32, (LANES, tk), 0) == blk_of_col).astype(sel_scr.dtype)
        allowed = jnp.dot(sel_scr[...], onehot, preferred_element_type=F32)
        s_m = jnp.dot(qbm_scr[...], kt_step(mk_pages), preferred_element_type=F32) + page_bias(0, rel, moba_cols)
        alpha, pexp = softmax_step(0, jnp.where(allowed > 0.5, s_m, NEG_INF))
        accm_scr[...] = alpha * accm_scr[...] + _dot_nt(pexp, kt_step(mv_pages))
        s_d = jnp.dot(qbd_scr[...], kt_step(dk_pages), preferred_element_type=F32) + page_bias(1, rel, diff_cols)
        alpha, pexp = softmax_step(1, s_d)
        diff_pv(alpha, pexp, lambda h: jnp.concatenate(
            [r[0, 0, pl.ds(h, PAGE_ROWS, stride=DIFF_HEADS), :] for r in dv_pages], axis=0).astype(MM))

        @pl.when(j == n_steps - 1)
        def _finish():
            ncol = kmn_ref.shape[1]
            col = lax.broadcasted_iota(jnp.int32, (rows, ncol), 1)
            reln = row_t - (col - b * tn)
            ok = (col >= b * tn) & (col < (b + 1) * tn) & (reln >= 0)
            s_mn = _dot(qbm_scr[...], kmn_ref[...]) + bias_rows(reln, moba_cols)
            alpha, pexp = softmax_step(0, jnp.where(ok, s_mn, NEG_INF))
            accm = alpha * accm_scr[...] + _dot_nt(pexp, vmn_ref[...])
            s_dn = _dot(qbd_scr[...], kdn_ref[...]) + bias_rows(reln, diff_cols)
            alpha, pexp = softmax_step(1, jnp.where(ok, s_dn, NEG_INF))
            diff_pv(alpha, pexp, lambda h: vdn_ref[:, h * DIFF_VDIM:(h + 1) * DIFF_VDIM].astype(MM))
            lam = lam_smem[0]
            om = accm / l_scr[0]
            for h in range(MOBA_HEADS):
                om_ref[0, :, h * HEAD_DIM:(h + 1) * HEAD_DIM] = (
                    om[h * TOKEN_ROWS:h * TOKEN_ROWS + tn, h * HEAD_DIM:(h + 1) * HEAD_DIM])
            od = accd_scr[...] / l_scr[1]
            for h in range(DIFF_HEADS):
                o0 = od[2 * h * TOKEN_ROWS:2 * h * TOKEN_ROWS + tn]
                o1 = od[(2 * h + 1) * TOKEN_ROWS:(2 * h + 1) * TOKEN_ROWS + tn]
                od_ref[0, :, h * DIFF_VDIM:(h + 1) * DIFF_VDIM] = o0 - lam * o1


def _sample_attn(qm, qd, kmn, vmn, kdn, vdn, caches, page_table, layer, rel_bias, lam, tn):
    nb = qm.shape[0]
    n_pages = page_table.shape[1]
    n_steps = n_pages // PAGES_PER_STEP
    ncol = kmn.shape[1]
    qspec = pl.BlockSpec((1, SROWS, MOBA_W), lambda b, ph, j, pt: (b, 0, 0))
    newt = pl.BlockSpec((MOBA_W, ncol), lambda b, ph, j, pt: (0, 0))
    newr = pl.BlockSpec((ncol, DIFF_W), lambda b, ph, j, pt: (0, 0))
    out = pl.BlockSpec((1, tn, MOBA_W), lambda b, ph, j, pt: (b, 0, 0))
    smem = pl.BlockSpec(memory_space=pltpu.SMEM)

    def page_specs(cache, both_phases):
        zeros = (0,) * (cache.ndim - 2)

        def index_map(b, ph, j, pt, g):
            step = j if both_phases else j * ph
            return (layer, pt[b, step * PAGES_PER_STEP + g]) + zeros
        return [pl.BlockSpec((1, 1) + cache.shape[2:], functools.partial(index_map, g=g))
                for g in range(PAGES_PER_STEP)]

    mk, mv, dk, dv = caches
    grid_spec = pltpu.PrefetchScalarGridSpec(
        num_scalar_prefetch=1, grid=(nb, 2, n_steps),
        in_specs=([smem, smem, qspec, qspec, newt, newt, newt, newr]
                  + page_specs(mk, True) + page_specs(mv, False) + page_specs(dk, False) + page_specs(dv, False)),
        out_specs=[out, out],
        scratch_shapes=[
            pltpu.VMEM((SROWS, MOBA_W), MM), pltpu.VMEM((SROWS, MOBA_W), MM),
            pltpu.VMEM((SROWS, LANES), F32), pltpu.VMEM((SROWS, LANES), MM),
            pltpu.VMEM((2, SROWS, 1), F32), pltpu.VMEM((2, SROWS, 1), F32), pltpu.VMEM((2, SROWS, 1), F32),
            pltpu.VMEM((SROWS, MOBA_W), F32), pltpu.VMEM((SROWS, DIFF_VDIM), F32),
        ])
    args = []
    for c in caches:
        args += [c] * PAGES_PER_STEP
    return pl.pallas_call(
        functools.partial(_sattn_kernel, tn=tn), grid_spec=grid_spec,
        out_shape=[jax.ShapeDtypeStruct((nb, tn, MOBA_W), F32), jax.ShapeDtypeStruct((nb, tn, DIFF_W), F32)],
        compiler_params=_cparams("arbitrary", "arbitrary", "arbitrary"), name="sattn",
    )(page_table, rel_bias, lam, qm, qd, kmn, vmn, kdn, vdn, *args)


def _merge_kernel(sub_scale, h_ref, om_ref, od_ref, gsub_ref, wout_ref, gffn_ref, wpq_ref, h1_ref, xn_ref, q_ref):
    od = od_ref[...]
    parts = [om_ref[...]]
    for h in range(DIFF_HEADS):
        parts.append(_rms(od[:, h * DIFF_VDIM:(h + 1) * DIFF_VDIM], gsub_ref[...]) * sub_scale)
    y = jnp.concatenate(parts, axis=-1)
    h1 = h_ref[...] + _dot(y, wout_ref[...])
    h1_ref[...] = h1
    xn = _rms(h1, gffn_ref[...])
    xn_ref[...] = xn.astype(xn_ref.dtype)
    q_ref[...] = _dot(xn, wpq_ref[...]).astype(q_ref.dtype)


def _merge(h, om, od, g_sub, w_out, g_ffn, w_pq, sub_scale, tt):
    t, d = h.shape
    nq = w_pq.shape[1]
    row = lambda i: (i, 0)
    const = lambda i: (0, 0)
    return pl.pallas_call(
        functools.partial(_merge_kernel, sub_scale),
        grid=(t // tt,),
        in_specs=[pl.BlockSpec((tt, d), row), pl.BlockSpec((tt, MOBA_W), row), pl.BlockSpec((tt, DIFF_W), row),
                  pl.BlockSpec((1, DIFF_VDIM), const), pl.BlockSpec(w_out.shape, const),
                  pl.BlockSpec((1, d), const), pl.BlockSpec(w_pq.shape, const)],
        out_specs=[pl.BlockSpec((tt, d), row), pl.BlockSpec((tt, d), row), pl.BlockSpec((tt, nq), row)],
        out_shape=[jax.ShapeDtypeStruct((t, d), F32), jax.ShapeDtypeStruct((t, d), MM),
                   jax.ShapeDtypeStruct((t, nq), MM)],
        compiler_params=_cparams("parallel"), name="merge",
    )(h, om, od, g_sub, w_out, g_ffn, w_pq)


def _topk_rows(s, k):
    n = s.shape[0]
    row = lax.broadcasted_iota(jnp.int32, s.shape, 0)
    rank = jnp.full(s.shape, k, jnp.int32)
    vals = jnp.zeros((k, s.shape[1]), F32)
    vrow = lax.broadcasted_iota(jnp.int32, vals.shape, 0)
    for r in range(k):
        m = jnp.max(s, axis=0, keepdims=True)
        idx = jnp.min(jnp.where(s == m, row, n), axis=0, keepdims=True)
        hit = row == idx
        rank = jnp.where(hit, r, rank)
        s = jnp.where(hit, NEG_INF, s)
        vals = jnp.where(vrow == r, m, vals)
    return vals, rank


_CAND_FULL = 8


def _peersel_kernel(q_ref, keys_ref, rank2_ref, cnt_ref, e1_ref, e2_ref):
    k = PEER_TOPK
    q = q_ref[...]
    s1 = _dot_nt(keys_ref[0, 0], q[:, :PEER_HALF])
    s2 = _dot_nt(keys_ref[0, 1], q[:, PEER_HALF:])
    v1, rank1 = _topk_rows(s1, k)
    v2, rank2 = _topk_rows(s2, k)
    slabs = [v1[r:r + 1] + v2 for r in range(_CAND_FULL)]
    slabs.append(v1[_CAND_FULL:] + v2[0:1])
    cand = jnp.concatenate(slabs, axis=0)
    picked, crank = _topk_rows(cand, k)
    taken = (crank < k).astype(F32)
    z = jnp.sum(jnp.exp(picked - picked[0:1]), axis=0, keepdims=True)
    cnt = jnp.zeros(s1.shape, F32)
    for r in range(k):
        if r < _CAND_FULL:
            c_r = jnp.sum(taken[r * k:(r + 1) * k], axis=0, keepdims=True)
        else:
            c_r = taken[_CAND_FULL * k + r - _CAND_FULL:_CAND_FULL * k + r - _CAND_FULL + 1]
        cnt = jnp.where(rank1 == r, c_r, cnt)
    rank2_ref[0] = rank2.astype(rank2_ref.dtype)
    cnt_ref[0] = cnt.astype(cnt_ref.dtype)
    e1_ref[0] = (jnp.exp(s1 - v1[0:1]) / z).astype(e1_ref.dtype)
    e2_ref[0] = jnp.exp(s2 - v2[0:1]).astype(e2_ref.dtype)


def _peersel(q, sub_keys, tt):
    t = q.shape[0]
    spec = pl.BlockSpec((1, PEER_NKEYS, tt), lambda i, h: (h, 0, i))
    tile = jax.ShapeDtypeStruct((PEER_HEADS, PEER_NKEYS, t), MM)
    rows = jax.ShapeDtypeStruct((PEER_HEADS, PEER_NKEYS, t), F32)
    return pl.pallas_call(
        _peersel_kernel,
        grid=(t // tt, PEER_HEADS),
        in_specs=[pl.BlockSpec((tt, 2 * PEER_HALF), lambda i, h: (i, h)),
                  pl.BlockSpec((1, 2, PEER_NKEYS, PEER_HALF), lambda i, h: (h, 0, 0, 0))],
        out_specs=[spec] * 4, out_shape=[tile, rows, rows, tile],
        compiler_params=_cparams("parallel", "arbitrary"), name="peersel",
    )(q, sub_keys)


PEER_I1_PER_STEP = 4


def _peerffn_kernel(xn_ref, u_ref, v_ref, rank2_ref, cnt_ref, e1_ref, e2_ref, out_ref, acc_scr):
    j = pl.program_id(1)

    @pl.when(j == 0)
    def _zero():
        acc_scr[...] = jnp.zeros(acc_scr.shape, F32)

    hid = _dot_nt(u_ref[...], xn_ref[...])
    act = 0.5 * hid * (1.0 + lax.erf(hid * (2.0 ** -0.5)))
    tt = xn_ref.shape[0]
    wdt = rank2_ref.dtype
    pack = 4 // jnp.dtype(wdt).itemsize * SUBLANES
    grp = (PEER_NKEYS // pack, pack, tt)
    blocks = []
    for a in range(PEER_I1_PER_STEP):
        i1 = j * PEER_I1_PER_STEP + a
        w = jnp.zeros(grp, wdt)
        for h in range(PEER_HEADS):
            cnt = jnp.broadcast_to(cnt_ref[h, pl.ds(i1, 1), :], (pack, tt)).astype(wdt)
            e1 = jnp.broadcast_to(e1_ref[h, pl.ds(i1, 1), :], (pack, tt)).astype(wdt)
            w = w + jnp.where(rank2_ref[h].reshape(grp) < cnt, e2_ref[h].reshape(grp), jnp.zeros_like(w)) * e1
        blocks.append(w.reshape(PEER_NKEYS, tt) * act[a * PEER_NKEYS:(a + 1) * PEER_NKEYS].astype(wdt))
    coef = jnp.concatenate(blocks, axis=0)
    acc_scr[...] += _dot_tn(v_ref[...], coef)

    @pl.when(j == pl.num_programs(1) - 1)
    def _store():
        out_ref[...] = acc_scr[...].T


def _peerffn(xn, u, v, rank2, cnt, e1, e2, tt):
    t, d = xn.shape
    ne = PEER_I1_PER_STEP * PEER_NKEYS
    sel = pl.BlockSpec((PEER_HEADS, PEER_NKEYS, tt), lambda i, j: (0, 0, i))
    return pl.pallas_call(
        _peerffn_kernel,
        grid=(t // tt, PEER_NKEYS // PEER_I1_PER_STEP),
        in_specs=[pl.BlockSpec((tt, d), lambda i, j: (i, 0)),
                  pl.BlockSpec((ne, d), lambda i, j: (j, 0)), pl.BlockSpec((ne, d), lambda i, j: (j, 0)),
                  sel, sel, sel, sel],
        out_specs=pl.BlockSpec((tt, d), lambda i, j: (i, 0)),
        out_shape=jax.ShapeDtypeStruct((t, d), F32),
        scratch_shapes=[pltpu.VMEM((d, tt), F32)],
        compiler_params=_cparams("parallel", "arbitrary"), name="peerffn",
    )(xn, u, v, rank2, cnt, e1, e2)


def _ple_kernel(h_ref, peer_ref, p_ref, g_ref, wg_ref, wp_ref, out_ref):
    h2 = h_ref[...] + peer_ref[...]
    gate = jax.nn.sigmoid(_dot(_rms(h2, g_ref[...]), wg_ref[...]))
    out_ref[...] = h2 + gate * _dot(p_ref[...], wp_ref[...])


def _ple(h1, peer, p, g_ple, w_pg, w_pp, tt):
    t, d = h1.shape
    row = lambda i: (i, 0)
    const = lambda i: (0, 0)
    return pl.pallas_call(
        _ple_kernel,
        grid=(t // tt,),
        in_specs=[pl.BlockSpec((tt, d), row), pl.BlockSpec((tt, d), row), pl.BlockSpec((tt, p.shape[1]), row),
                  pl.BlockSpec((1, d), const), pl.BlockSpec(w_pg.shape, const), pl.BlockSpec(w_pp.shape, const)],
        out_specs=pl.BlockSpec((tt, d), row),
        out_shape=jax.ShapeDtypeStruct((t, d), F32),
        compiler_params=_cparams("parallel"), name="ple",
    )(h1, peer, p, g_ple, w_pg, w_pp)


def _token_tile(t, pref):
    return pref if t % pref == 0 else t


def kernel(x_prompt, x_sample, cache_moba_k, cache_moba_v, cache_diff_k, cache_diff_v, page_table, p_prompt, p_sample, rel_bias, g_attn, w_in, g_q_moba, g_k_moba, g_q_diff, g_k_diff, lam_q1, lam_k1, lam_q2, lam_k2, g_subln, w_out, g_ffn, w_peer_q, peer_sub_keys, peer_u, peer_v, g_ple, w_ple_gate, w_ple_proj):
    bsz, seq, d = x_prompt.shape
    nb, tn, _ = x_sample.shape
    depth = w_in.shape[0]
    page = cache_moba_k.shape[2]
    assert page == PAGE_ROWS and seq % MOBA_BLOCK == 0 and seq // MOBA_BLOCK <= LANES and tn <= TOKEN_ROWS
    past_len = page_table.shape[1] * page
    assert past_len % (PAGES_PER_STEP * page) == 0 and past_len // MOBA_BLOCK <= LANES

    seg = jnp.asarray(np.kron(np.eye(MOBA_W // HEAD_DIM), np.full((HEAD_DIM, HEAD_DIM), 1.0 / HEAD_DIM)), jnp.bfloat16)
    caches = (jnp.transpose(cache_moba_k, (0, 1, 3, 4, 2)), jnp.transpose(cache_moba_v, (0, 1, 3, 4, 2)),
              jnp.transpose(cache_diff_k, (0, 1, 3, 4, 5, 2)),
              cache_diff_v.reshape(depth, cache_diff_v.shape[1], page * DIFF_HEADS, DIFF_VDIM))
    tile8 = lambda g: jnp.tile(g, MOBA_W // HEAD_DIM)

    hp = x_prompt.reshape(bsz * seq, d)
    hs = x_sample.reshape(nb * tn, d)
    new_p = [[] for _ in range(4)]
    new_s = [[] for _ in range(4)]
    w0, w1, w2, w3, w4, w5 = (slice(c * MOBA_W, (c + 1) * MOBA_W) for c in range(6))
    for i in range(depth):
        lam_init = 0.8 - 0.6 * math.exp(-0.3 * i)
        lam = (jnp.exp(jnp.sum(lam_q1[i] * lam_k1[i])) - jnp.exp(jnp.sum(lam_q2[i] * lam_k2[i])) + lam_init)
        lam = lam.reshape(1).astype(F32)
        wi = w_in[i]
        wkv = jnp.concatenate([wi[:, w1], wi[:, w4], wi[:, w2], wi[:, w5]], axis=1).astype(MM)
        wqt = jnp.concatenate([wi[:, w0], wi[:, w3]], axis=1).T.astype(MM)
        wvt = jnp.concatenate([wi[:, w2], wi[:, w5]], axis=1).T.astype(MM)
        gk = jnp.concatenate([tile8(g_k_moba[i]), tile8(g_k_diff[i])])[None, :]
        gqt = (jnp.concatenate([tile8(g_q_moba[i]), tile8(g_q_diff[i])]) * (ATTN_SCALE * LOG2E))[:, None]
        w_out_i = w_out[i].astype(MM)
        w_pq_i = w_peer_q[i].astype(MM)
        u_i = peer_u[i].astype(MM)
        v_i = peer_v[i].astype(MM)
        w_pg_i = w_ple_gate[i].astype(MM)
        w_pp_i = w_ple_proj[i].astype(MM)

        tp = _token_tile(bsz * seq, 512)
        km, vm, kd, vd, kb, qt, vt, kmean = _proj(hp, g_attn[i][None], wkv, wqt, wvt, seg, gk, gqt, tp, True)
        for lst, a in zip(new_p, (km, vm, kd, vd)):
            lst.append(a)
        kmean = kmean.reshape(bsz, seq // MOBA_BLOCK, MOBA_W)
        kmean = jnp.pad(kmean, ((0, 0), (0, LANES - kmean.shape[1]), (0, 0)))
        om_p, od_p = _prompt_attn(qt, kb, vt, kmean, rel_bias, lam, bsz)

        ts = _token_tile(nb * tn, 128)
        km, vm, kd, vd, _, qt, _ = _proj(hs, g_attn[i][None], wkv, wqt, wvt, seg, gk, gqt, ts, False)
        for lst, a in zip(new_s, (km, vm, kd, vd)):
            lst.append(a)
        qs = qt.T.astype(F32).reshape(nb, 1, tn, QK_W)
        qs = jnp.pad(qs, ((0, 0), (0, 0), (0, TOKEN_ROWS - tn), (0, 0)))
        qs = jnp.broadcast_to(qs, (nb, MAPS_PER_TOKEN, TOKEN_ROWS, QK_W)).reshape(nb, SROWS, QK_W)
        om_s, od_s = _sample_attn(qs[:, :, :MOBA_W], qs[:, :, MOBA_W:], km.T, vm.T, kd.T, vd, caches, page_table, i,
                                  rel_bias, lam, tn)

        outs = []
        for h, om, od, p, tt in ((hp, om_p, od_p, p_prompt[i].reshape(bsz * seq, -1), tp),
                                 (hs, om_s.reshape(-1, MOBA_W), od_s.reshape(-1, DIFF_W),
                                  p_sample[i].reshape(nb * tn, -1), ts)):
            h1, xn, q = _merge(h, om, od, g_subln[i][None], w_out_i, g_ffn[i][None], w_pq_i, 1.0 - lam_init, tt)
            rank2, cnt, e1, e2 = _peersel(q, peer_sub_keys[i], tt)
            peer = _peerffn(xn, u_i, v_i, rank2, cnt, e1, e2, tt)
            outs.append(_ple(h1, peer, p, g_ple[i][None], w_pg_i, w_pp_i, tt))
        hp, hs = outs

    def stack(lst, lead, tail):
        return jnp.stack([a.reshape(*lead, *tail) for a in lst])

    lp, ls = (bsz, seq), (nb, tn)
    tails = ((MOBA_HEADS, HEAD_DIM), (MOBA_HEADS, HEAD_DIM), (DIFF_HEADS, 2, HEAD_DIM), (DIFF_HEADS, DIFF_VDIM))
    return (hp.reshape(bsz, seq, d), hs.reshape(nb, tn, d),
            *[stack(l, lp, t) for l, t in zip(new_p, tails)],
            *[stack(l, ls, t) for l, t in zip(new_s, tails)])
```

```python
import functools
import math

import numpy as np
import jax
import jax.numpy as jnp
from jax import lax
from jax.experimental import pallas as pl
from jax.experimental.pallas import tpu as pltpu

F32 = jnp.float32
MM = jnp.bfloat16

HEAD_DIM = 64
MOBA_HEADS = 8
MOBA_BLOCK = 256
MOBA_TOPK = 3
DIFF_HEADS = 4
DIFF_VDIM = 2 * HEAD_DIM
MOBA_W = MOBA_HEADS * HEAD_DIM
DIFF_W = DIFF_HEADS * DIFF_VDIM
QK_W = MOBA_W + DIFF_W
N_MAPS = MOBA_HEADS + 2 * DIFF_HEADS
ATTN_SCALE = HEAD_DIM ** -0.5
LOG2E = math.log2(math.e)
REL_BUCKETS = 32
REL_MAX_EXACT = 16
REL_MAX_DIST = 128
PEER_HEADS = 8
PEER_NKEYS = 128
PEER_HALF = 128
PEER_TOPK = 16
RMS_EPS = 1e-6
LANES = 128
SUBLANES = 8
NEG_INF = float("-inf")
VMEM_LIMIT = 56 * 1024 * 1024


def _cparams(*sem):
    return pltpu.CompilerParams(dimension_semantics=sem, vmem_limit_bytes=VMEM_LIMIT)


def _rms(x, g):
    return x * lax.rsqrt(jnp.mean(x * x, axis=-1, keepdims=True) + RMS_EPS) * g


def _dot(a, b):
    return jnp.dot(a.astype(MM), b.astype(MM), preferred_element_type=F32)


def _dot_nt(a, b):
    return lax.dot_general(a.astype(MM), b.astype(MM), (((1,), (1,)), ((), ())), preferred_element_type=F32)


def _dot_tn(a, b):
    return lax.dot_general(a.astype(MM), b.astype(MM), (((0,), (0,)), ((), ())), preferred_element_type=F32)


def _seg_mean(sq, seg):
    hi = sq.astype(jnp.bfloat16)
    lo = (sq - hi.astype(F32)).astype(jnp.bfloat16)
    return (jnp.dot(hi, seg, preferred_element_type=F32) + jnp.dot(lo, seg, preferred_element_type=F32))


def _rel_bucket(rel):
    n = jnp.maximum(rel, 0)
    nf = jnp.maximum(n, 1).astype(F32)
    large = REL_MAX_EXACT + (jnp.log(nf / REL_MAX_EXACT) / math.log(REL_MAX_DIST / REL_MAX_EXACT)
                             * (REL_BUCKETS - REL_MAX_EXACT)).astype(jnp.int32)
    large = jnp.minimum(large, REL_BUCKETS - 1)
    return jnp.where(n < REL_MAX_EXACT, n, large)


def _bias_col(m):
    return m if m < MOBA_HEADS else MOBA_HEADS + (m - MOBA_HEADS) // 2


def _proj_kernel(x_ref, g_ref, wkv_ref, wqt_ref, wvt_ref, seg_ref, gk_ref, gqt_ref,
                 kmt_ref, vmt_ref, kdt_ref, vd_ref, kb_ref, qt_ref, vt_ref, *maybe_kmean):
    xn = _rms(x_ref[...], g_ref[...]).astype(MM)
    z = jnp.dot(xn, wkv_ref[...], preferred_element_type=F32)
    seg = seg_ref[...]
    gk = gk_ref[...]
    ks = []
    for c in range(2):
        zc = z[:, c * MOBA_W:(c + 1) * MOBA_W]
        ks.append(zc * lax.rsqrt(_seg_mean(zc * zc, seg) + RMS_EPS) * gk[:, c * MOBA_W:(c + 1) * MOBA_W])
    kmt_ref[0] = ks[0].T
    kdt_ref[0] = ks[1].T
    kb_ref[...] = jnp.concatenate(ks, axis=-1).astype(kb_ref.dtype)
    vmt_ref[0] = z[:, 2 * MOBA_W:3 * MOBA_W].T
    vd_ref[...] = z[:, 3 * MOBA_W:4 * MOBA_W]
    tt = xn.shape[0]
    zq = _dot_nt(wqt_ref[...], xn).reshape(QK_W // HEAD_DIM, HEAD_DIM, tt)
    ms = jnp.mean(zq * zq, axis=1, keepdims=True)
    qt = (zq * lax.rsqrt(ms + RMS_EPS)).reshape(QK_W, tt) * gqt_ref[...]
    qt_ref[...] = qt.astype(qt_ref.dtype)
    vt_ref[...] = _dot_nt(wvt_ref[...], xn).astype(vt_ref.dtype)
    if maybe_kmean:
        kmean_ref, = maybe_kmean
        for c in range(tt // MOBA_BLOCK):
            kmean_ref[0, c:c + 1, :] = jnp.mean(ks[0][c * MOBA_BLOCK:(c + 1) * MOBA_BLOCK], axis=0, keepdims=True)


def _proj(x2d, g_attn, wkv, wqt, wvt, seg, gk, gqt, tt, n_seq, with_kmean):
    t, d = x2d.shape
    nt = t // tt
    tps = nt // n_seq
    row = lambda i: (i, 0)
    col = lambda i: (0, i)
    const = lambda i: (0, 0)
    full = lambda a: pl.BlockSpec(a.shape, const)
    feat = jax.ShapeDtypeStruct((n_seq, MOBA_W, t // n_seq), F32)
    feat_spec = pl.BlockSpec((1, MOBA_W, tt), lambda i: (i // tps, 0, i % tps))
    out_shape = [feat, feat, feat, jax.ShapeDtypeStruct((t, MOBA_W), F32),
                 jax.ShapeDtypeStruct((t, QK_W), MM), jax.ShapeDtypeStruct((QK_W, t), MM),
                 jax.ShapeDtypeStruct((QK_W, t), MM)]
    out_specs = [feat_spec, feat_spec, feat_spec, pl.BlockSpec((tt, MOBA_W), row),
                 pl.BlockSpec((tt, QK_W), row), pl.BlockSpec((QK_W, tt), col), pl.BlockSpec((QK_W, tt), col)]
    if with_kmean:
        nb = tt // MOBA_BLOCK
        out_shape.append(jax.ShapeDtypeStruct((nt, nb, MOBA_W), F32))
        out_specs.append(pl.BlockSpec((1, nb, MOBA_W), lambda i: (i, 0, 0)))
    return pl.pallas_call(
        _proj_kernel,
        grid=(nt,),
        in_specs=[pl.BlockSpec((tt, d), row), full(g_attn), full(wkv), full(wqt), full(wvt), full(seg), full(gk),
                  full(gqt)],
        out_specs=out_specs, out_shape=out_shape,
        compiler_params=_cparams("parallel"), name="proj",
    )(x2d, g_attn, wkv, wqt, wvt, seg, gk, gqt)


def _top_select(gate, pos, axis, n_sel, n_ok):
    sel = jnp.zeros(gate.shape, F32)
    big = gate.shape[axis]
    for k in range(n_sel):
        m = jnp.max(gate, axis=axis, keepdims=True)
        idx = jnp.min(jnp.where(gate == m, pos, big), axis=axis, keepdims=True)
        hit = pos == idx
        sel = jnp.where(hit & (n_ok > k), 1.0, sel)
        gate = jnp.where(hit, NEG_INF, gate)
    return sel


def _attn_kernel(qi_ref, kj_ref, last_ref, bias_smem, lam_smem,
                 qt_ref, kb_ref, vt_ref, kmean_ref,
                 om_ref, od_ref,
                 bias_scr, qp_scr, sel_scr, m_scr, l_scr, accm_scr, accd_scr, s_scr):
    b = pl.program_id(0)
    p = pl.program_id(1)
    qi = qi_ref[p]
    kj = kj_ref[p]
    d = qi - kj
    tq = qt_ref.shape[1]
    tk = kb_ref.shape[0]

    @pl.when((b == 0) & (p == 0))
    def _build_bias():
        c = lax.broadcasted_iota(jnp.int32, (tk, tq), 0)
        r = lax.broadcasted_iota(jnp.int32, (tk, tq), 1)
        for dd in range(2):
            rel = dd * tq + r - c
            bucket = _rel_bucket(rel)
            for m in range(N_MAPS):
                bt = jnp.zeros((tk, tq), F32)
                for k in range(REL_BUCKETS):
                    bt = jnp.where(bucket == k, bias_smem[k, _bias_col(m)], bt)
                if dd == 0:
                    bt = jnp.where(rel >= 0, bt, NEG_INF)
                bias_scr[m, dd] = bt * LOG2E
        for m in range(N_MAPS):
            bias_scr[m, 2] = jnp.full((tk, tq), bias_smem[REL_BUCKETS - 1, _bias_col(m)], F32) * LOG2E

    @pl.when(d == 0)
    def _new_q_tile():
        half = lax.broadcasted_iota(jnp.int32, (2 * HEAD_DIM, tq), 0) // HEAD_DIM
        for j in range(N_MAPS // 2):
            qpair = qt_ref[j * 2 * HEAD_DIM:(j + 1) * 2 * HEAD_DIM, :].astype(F32)
            for e in range(2):
                qp_scr[2 * j + e] = jnp.where(half == e, qpair, 0.0).astype(qp_scr.dtype)
        blk = lax.broadcasted_iota(jnp.int32, (kmean_ref.shape[1], tq), 0)
        qblk = qi
        kmean = kmean_ref[0]
        for h in range(MOBA_HEADS):
            j = h // 2
            gate = _dot(kmean[:, j * 2 * HEAD_DIM:(j + 1) * 2 * HEAD_DIM], qp_scr[h])
            gate = jnp.where(blk < qblk, gate, NEG_INF)
            sel = _top_select(gate, blk, 0, MOBA_TOPK, qblk)
            sel_scr[h] = jnp.where(blk == qblk, 1.0, sel)
        m_scr[...] = jnp.full(m_scr.shape, NEG_INF, F32)
        l_scr[...] = jnp.zeros(l_scr.shape, F32)
        accm_scr[...] = jnp.zeros(accm_scr.shape, F32)
        accd_scr[...] = jnp.zeros(accd_scr.shape, F32)

    def scores(m):
        j = m // 2
        k_pair = kb_ref[:, j * 2 * HEAD_DIM:(j + 1) * 2 * HEAD_DIM]
        return jnp.dot(k_pair, qp_scr[m], preferred_element_type=F32)

    dsel = jnp.minimum(d, 2)
    ones = jnp.ones((2 * SUBLANES, tk), MM)
    n_buf = s_scr.shape[0]
    for m in range(n_buf - 1):
        s_scr[m] = scores(m)
    for m in range(N_MAPS):
        if m + n_buf - 1 < N_MAPS:
            s_scr[(m + n_buf - 1) % n_buf] = scores(m + n_buf - 1)
        s = s_scr[m % n_buf] + bias_scr[m, dsel]
        if m < MOBA_HEADS:
            s = jnp.where(sel_scr[m, pl.ds(kj, 1), :] > 0.5, s, NEG_INF)
            vt = vt_ref[m * HEAD_DIM:(m + 1) * HEAD_DIM, :]
        else:
            hv = (m - MOBA_HEADS) // 2
            vt = vt_ref[MOBA_W + hv * DIFF_VDIM:MOBA_W + (hv + 1) * DIFF_VDIM, :]
        m_prev = m_scr[m]
        m_new = jnp.maximum(m_prev, jnp.max(s, axis=0, keepdims=True))
        alpha = jnp.exp2(m_prev - m_new)
        pexp = jnp.exp2((s - m_new).astype(MM))
        m_scr[m] = m_new
        dv = vt.shape[0]
        pv = jnp.dot(jnp.concatenate([vt, ones], axis=0), pexp, preferred_element_type=F32)
        l_scr[m] = alpha * l_scr[m] + pv[dv:dv + 1]
        if m < MOBA_HEADS:
            accm_scr[m] = alpha * accm_scr[m] + pv[:dv]
        else:
            accd_scr[m - MOBA_HEADS] = alpha * accd_scr[m - MOBA_HEADS] + pv[:dv]

    @pl.when(last_ref[p] == 1)
    def _finish():
        lam = lam_smem[0]
        omt = jnp.concatenate([accm_scr[h] / l_scr[h] for h in range(MOBA_HEADS)], axis=0)
        om_ref[...] = omt.T
        outs = []
        for h in range(DIFF_HEADS):
            o0 = accd_scr[2 * h] / l_scr[MOBA_HEADS + 2 * h]
            o1 = accd_scr[2 * h + 1] / l_scr[MOBA_HEADS + 2 * h + 1]
            outs.append(o0 - lam * o1)
        od_ref[...] = jnp.concatenate(outs, axis=0).T


SCORE_BUFS = 5


def _attn_schedule(nq):
    qi, kj, last = [], [], []
    for i in range(nq):
        order = [i] + list(range(i))
        for n, j in enumerate(order):
            qi.append(i)
            kj.append(j)
            last.append(1 if n == len(order) - 1 else 0)
    return (jnp.asarray(qi, jnp.int32), jnp.asarray(kj, jnp.int32), jnp.asarray(last, jnp.int32))


def _prompt_attn(qt, kb, vt, kmean, rel_bias, lam, bsz):
    nbp = kmean.shape[1]
    s = kb.shape[0] // bsz
    tq = tk = MOBA_BLOCK
    nq = s // tq
    qi, kj, last = _attn_schedule(nq)
    qcol = pl.BlockSpec((QK_W, tq), lambda b, p, qi, kj, last: (0, b * nq + qi[p]))
    kcol = pl.BlockSpec((QK_W, tk), lambda b, p, qi, kj, last: (0, b * nq + kj[p]))
    krow = pl.BlockSpec((tk, QK_W), lambda b, p, qi, kj, last: (b * nq + kj[p], 0))
    orow = pl.BlockSpec((tq, MOBA_W), lambda b, p, qi, kj, last: (b * nq + qi[p], 0))
    smem = pl.BlockSpec(memory_space=pltpu.SMEM)
    grid_spec = pltpu.PrefetchScalarGridSpec(
        num_scalar_prefetch=3,
        grid=(bsz, int(qi.shape[0])),
        in_specs=[smem, smem, qcol, krow, kcol,
                  pl.BlockSpec((1, nbp, MOBA_W), lambda b, p, qi, kj, last: (b, 0, 0))],
        out_specs=[orow, orow],
        scratch_shapes=[
            pltpu.VMEM((N_MAPS, 3, tk, tq), F32),
            pltpu.VMEM((N_MAPS, 2 * HEAD_DIM, tq), MM),
            pltpu.VMEM((MOBA_HEADS, nbp, tq), F32),
            pltpu.VMEM((N_MAPS, 1, tq), F32),
            pltpu.VMEM((N_MAPS, 1, tq), F32),
            pltpu.VMEM((MOBA_HEADS, HEAD_DIM, tq), F32),
            pltpu.VMEM((2 * DIFF_HEADS, DIFF_VDIM, tq), F32),
            pltpu.VMEM((SCORE_BUFS, tk, tq), F32),
        ])
    t = kb.shape[0]
    return pl.pallas_call(
        _attn_kernel, grid_spec=grid_spec,
        out_shape=[jax.ShapeDtypeStruct((t, MOBA_W), F32), jax.ShapeDtypeStruct((t, DIFF_W), F32)],
        compiler_params=_cparams("arbitrary", "arbitrary"), name="attn",
    )(qi, kj, last, rel_bias, lam, qt, kb, vt, kmean)


PAGES_PER_STEP = 8
PAGE_ROWS = 128
TOKEN_ROWS = SUBLANES
MAPS_PER_TOKEN = 8
SROWS = MAPS_PER_TOKEN * TOKEN_ROWS


def _sattn_kernel(pt_ref, bias_smem, lam_smem,
                  qm_ref, qd_ref, kmn_ref, vmn_ref, kdn_ref, vdn_ref, *refs, tn):
    n = PAGES_PER_STEP
    mk_pages, mv_pages = refs[0:n], refs[n:2 * n]
    dk_pages, dv_pages = refs[2 * n:3 * n], refs[3 * n:4 * n]
    om_ref, od_ref = refs[4 * n], refs[4 * n + 1]
    qbm_scr, qbd_scr, gate_scr, sel_scr, m_scr, l_scr, far_scr, accm_scr, accd_scr = refs[4 * n + 2:]
    b = pl.program_id(0)
    ph = pl.program_id(1)
    j = pl.program_id(2)
    n_steps = pl.num_programs(2)
    tk = n * PAGE_ROWS
    past_len = n_steps * tk
    bps = tk // MOBA_BLOCK
    n_past = past_len // MOBA_BLOCK
    rows = SROWS

    row_m = lax.broadcasted_iota(jnp.int32, (rows, 1), 0) // TOKEN_ROWS
    row_t = lax.broadcasted_iota(jnp.int32, (rows, 1), 0) % TOKEN_ROWS
    moba_cols = [_bias_col(m) for m in range(MOBA_HEADS)]
    diff_cols = [_bias_col(MOBA_HEADS + m) for m in range(2 * DIFF_HEADS)]

    def bias_rows(rel, cols):
        bucket = _rel_bucket(rel)
        out = jnp.zeros(rel.shape, F32)
        for k in range(REL_BUCKETS):
            tab = jnp.zeros((rows, 1), F32)
            for mm, cc in enumerate(cols):
                tab = jnp.where(row_m == mm, bias_smem[k, cc], tab)
            out = jnp.where(bucket == k, tab, out)
        return out * LOG2E

    def page_bias(branch, rel, cols):
        return lax.cond(j == n_steps - 1, lambda: bias_rows(rel, cols),
                        lambda: jnp.broadcast_to(far_scr[branch], rel.shape))

    def kt_step(pages):
        return jnp.concatenate([r[0, 0].reshape(MOBA_W, PAGE_ROWS) for r in pages], axis=-1).astype(MM)

    @pl.when((ph == 0) & (j == 0))
    def _start():
        def keep_own_segment(q):
            lane_seg = lax.broadcasted_iota(jnp.int32, q.shape, 1) // HEAD_DIM
            return jnp.where(lane_seg == row_m, q, 0.0)
        qbm_scr[...] = keep_own_segment(qm_ref[0]).astype(qbm_scr.dtype)
        qbd_scr[...] = keep_own_segment(qd_ref[0]).astype(qbd_scr.dtype)
        gate_scr[...] = jnp.full(gate_scr.shape, NEG_INF, F32)
        far = jnp.full((rows, 1), REL_MAX_DIST, jnp.int32)
        far_scr[0] = bias_rows(far, moba_cols)
        far_scr[1] = bias_rows(far, diff_cols)

    @pl.when(ph == 0)
    def _gate():
        s = jnp.dot(qbm_scr[...], kt_step(mk_pages), preferred_element_type=F32)
        col = lax.broadcasted_iota(jnp.int32, (rows, LANES), 1)
        g = gate_scr[...]
        for c in range(bps):
            bs = jnp.sum(s[:, c * MOBA_BLOCK:(c + 1) * MOBA_BLOCK], axis=-1, keepdims=True)
            g = jnp.where(col == j * bps + c, bs, g)
        gate_scr[...] = g

        @pl.when(j == n_steps - 1)
        def _select():
            sel = _top_select(g, col, 1, min(MOBA_TOPK, n_past), jnp.full((rows, 1), n_past, jnp.int32))
            sel_scr[...] = sel.astype(sel_scr.dtype)
            m_scr[...] = jnp.full(m_scr.shape, NEG_INF, F32)
            l_scr[...] = jnp.zeros(l_scr.shape, F32)
            accm_scr[...] = jnp.zeros(accm_scr.shape, F32)
            accd_scr[...] = jnp.zeros(accd_scr.shape, F32)

    def softmax_step(branch, s):
        m_prev = m_scr[branch]
        m_new = jnp.maximum(m_prev, jnp.max(s, axis=-1, keepdims=True))
        m_safe = jnp.where(m_new == NEG_INF, 0.0, m_new)
        alpha = jnp.exp2(m_prev - m_safe)
        pexp = jnp.exp2(s - m_safe)
        l_scr[branch] = alpha * l_scr[branch] + jnp.sum(pexp, axis=-1, keepdims=True)
        m_scr[branch] = m_new
        return alpha, pexp.astype(MM)

    def diff_pv(alpha, pexp, v_of_head):
        for h in range(DIFF_HEADS):
            rs = slice(h * 2 * TOKEN_ROWS, (h + 1) * 2 * TOKEN_ROWS)
            accd_scr[rs] = alpha[rs] * accd_scr[rs] + jnp.dot(pexp[rs], v_of_head(h), preferred_element_type=F32)

    @pl.when(ph == 1)
    def _attend():
        kpos = j * tk + lax.broadcasted_iota(jnp.int32, (rows, tk), 1)
        rel = past_len + row_t - kpos
        blk_of_col = (lax.broadcasted_iota(jnp.int32, (LANES, tk), 1) // MOBA_BLOCK) + j * bps
        onehot = (lax.broadcasted_iota(jnp.int32, (LANES, tk), 0) == blk_of_col).astype(sel_scr.dtype)
        allowed = jnp.dot(sel_scr[...], onehot, preferred_element_type=F32)
        s_m = jnp.dot(qbm_scr[...], kt_step(mk_pages), preferred_element_type=F32) + page_bias(0, rel, moba_cols)
        alpha, pexp = softmax_step(0, jnp.where(allowed > 0.5, s_m, NEG_INF))
        accm_scr[...] = alpha * accm_scr[...] + _dot_nt(pexp, kt_step(mv_pages))
        s_d = jnp.dot(qbd_scr[...], kt_step(dk_pages), preferred_element_type=F32) + page_bias(1, rel, diff_cols)
        alpha, pexp = softmax_step(1, s_d)
        diff_pv(alpha, pexp, lambda h: jnp.concatenate(
            [r[0, 0, pl.ds(h, PAGE_ROWS, stride=DIFF_HEADS), :] for r in dv_pages], axis=0).astype(MM))

        @pl.when(j == n_steps - 1)
        def _finish():
            ncol = kmn_ref.shape[1]
            col = lax.broadcasted_iota(jnp.int32, (rows, ncol), 1)
            reln = row_t - (col - b * tn)
            ok = (col >= b * tn) & (col < (b + 1) * tn) & (reln >= 0)
            s_mn = _dot(qbm_scr[...], kmn_ref[...]) + bias_rows(reln, moba_cols)
            alpha, pexp = softmax_step(0, jnp.where(ok, s_mn, NEG_INF))
            accm = alpha * accm_scr[...] + _dot_nt(pexp, vmn_ref[...])
            s_dn = _dot(qbd_scr[...], kdn_ref[...]) + bias_rows(reln, diff_cols)
            alpha, pexp = softmax_step(1, jnp.where(ok, s_dn, NEG_INF))
            diff_pv(alpha, pexp, lambda h: vdn_ref[:, h * DIFF_VDIM:(h + 1) * DIFF_VDIM].astype(MM))
            lam = lam_smem[0]
            om = accm / l_scr[0]
            for h in range(MOBA_HEADS):
                om_ref[0, :, h * HEAD_DIM:(h + 1) * HEAD_DIM] = (
                    om[h * TOKEN_ROWS:h * TOKEN_ROWS + tn, h * HEAD_DIM:(h + 1) * HEAD_DIM])
            od = accd_scr[...] / l_scr[1]
            for h in range(DIFF_HEADS):
                o0 = od[2 * h * TOKEN_ROWS:2 * h * TOKEN_ROWS + tn]
                o1 = od[(2 * h + 1) * TOKEN_ROWS:(2 * h + 1) * TOKEN_ROWS + tn]
                od_ref[0, :, h * DIFF_VDIM:(h + 1) * DIFF_VDIM] = o0 - lam * o1


def _sample_attn(qm, qd, kmn, vmn, kdn, vdn, caches, page_table, layer, rel_bias, lam, tn):
    nb = qm.shape[0]
    n_pages = page_table.shape[1]
    n_steps = n_pages // PAGES_PER_STEP
    ncol = kmn.shape[1]
    qspec = pl.BlockSpec((1, SROWS, MOBA_W), lambda b, ph, j, pt: (b, 0, 0))
    newt = pl.BlockSpec((MOBA_W, ncol), lambda b, ph, j, pt: (0, 0))
    newr = pl.BlockSpec((ncol, DIFF_W), lambda b, ph, j, pt: (0, 0))
    out = pl.BlockSpec((1, tn, MOBA_W), lambda b, ph, j, pt: (b, 0, 0))
    smem = pl.BlockSpec(memory_space=pltpu.SMEM)

    def page_specs(cache, both_phases):
        zeros = (0,) * (cache.ndim - 2)

        def index_map(b, ph, j, pt, g):
            step = j if both_phases else j * ph
            return (layer, pt[b, step * PAGES_PER_STEP + g]) + zeros
        return [pl.BlockSpec((1, 1) + cache.shape[2:], functools.partial(index_map, g=g))
                for g in range(PAGES_PER_STEP)]

    mk, mv, dk, dv = caches
    grid_spec = pltpu.PrefetchScalarGridSpec(
        num_scalar_prefetch=1, grid=(nb, 2, n_steps),
        in_specs=([smem, smem, qspec, qspec, newt, newt, newt, newr]
                  + page_specs(mk, True) + page_specs(mv, False) + page_specs(dk, False) + page_specs(dv, False)),
        out_specs=[out, out],
        scratch_shapes=[
            pltpu.VMEM((SROWS, MOBA_W), MM), pltpu.VMEM((SROWS, MOBA_W), MM),
            pltpu.VMEM((SROWS, LANES), F32), pltpu.VMEM((SROWS, LANES), MM),
            pltpu.VMEM((2, SROWS, 1), F32), pltpu.VMEM((2, SROWS, 1), F32), pltpu.VMEM((2, SROWS, 1), F32),
            pltpu.VMEM((SROWS, MOBA_W), F32), pltpu.VMEM((SROWS, DIFF_VDIM), F32),
        ])
    args = []
    for c in caches:
        args += [c] * PAGES_PER_STEP
    return pl.pallas_call(
        functools.partial(_sattn_kernel, tn=tn), grid_spec=grid_spec,
        out_shape=[jax.ShapeDtypeStruct((nb, tn, MOBA_W), F32), jax.ShapeDtypeStruct((nb, tn, DIFF_W), F32)],
        compiler_params=_cparams("arbitrary", "arbitrary", "arbitrary"), name="sattn",
    )(page_table, rel_bias, lam, qm, qd, kmn, vmn, kdn, vdn, *args)


def _merge_kernel(sub_scale, h_ref, om_ref, od_ref, gsub_ref, wout_ref, gffn_ref, wpq_ref, h1_ref, xn_ref, q_ref):
    od = od_ref[...]
    parts = [om_ref[...]]
    for h in range(DIFF_HEADS):
        parts.append(_rms(od[:, h * DIFF_VDIM:(h + 1) * DIFF_VDIM], gsub_ref[...]) * sub_scale)
    y = jnp.concatenate(parts, axis=-1)
    h1 = h_ref[...] + _dot(y, wout_ref[...])
    h1_ref[...] = h1
    xn = _rms(h1, gffn_ref[...])
    xn_ref[...] = xn.astype(xn_ref.dtype)
    q_ref[...] = _dot(xn, wpq_ref[...]).astype(q_ref.dtype)


def _merge(h, om, od, g_sub, w_out, g_ffn, w_pq, sub_scale, tt):
    t, d = h.shape
    nq = w_pq.shape[1]
    row = lambda i: (i, 0)
    const = lambda i: (0, 0)
    return pl.pallas_call(
        functools.partial(_merge_kernel, sub_scale),
        grid=(t // tt,),
        in_specs=[pl.BlockSpec((tt, d), row), pl.BlockSpec((tt, MOBA_W), row), pl.BlockSpec((tt, DIFF_W), row),
                  pl.BlockSpec((1, DIFF_VDIM), const), pl.BlockSpec(w_out.shape, const),
                  pl.BlockSpec((1, d), const), pl.BlockSpec(w_pq.shape, const)],
        out_specs=[pl.BlockSpec((tt, d), row), pl.BlockSpec((tt, d), row), pl.BlockSpec((tt, nq), row)],
        out_shape=[jax.ShapeDtypeStruct((t, d), F32), jax.ShapeDtypeStruct((t, d), MM),
                   jax.ShapeDtypeStruct((t, nq), MM)],
        compiler_params=_cparams("parallel"), name="merge",
    )(h, om, od, g_sub, w_out, g_ffn, w_pq)


def _topk_rows(s, k, want_rank):
    n = s.shape[0]
    row = lax.broadcasted_iota(jnp.int32, s.shape, 0)
    rank = jnp.full(s.shape, k, jnp.int32) if want_rank else None
    vals = jnp.zeros((k, s.shape[1]), F32)
    idxs = jnp.zeros((k, s.shape[1]), jnp.int32)
    vrow = lax.broadcasted_iota(jnp.int32, vals.shape, 0)
    for r in range(k):
        m = jnp.max(s, axis=0, keepdims=True)
        idx = jnp.min(jnp.where(s == m, row, n), axis=0, keepdims=True)
        hit = row == idx
        if want_rank:
            rank = jnp.where(hit, r, rank)
        s = jnp.where(hit, NEG_INF, s)
        vals = jnp.where(vrow == r, m, vals)
        idxs = jnp.where(vrow == r, idx, idxs)
    return vals, idxs, rank


def _top_pair_sums(v1, v2, k):
    row = lax.broadcasted_iota(jnp.int32, v1.shape, 0)
    p = jnp.zeros(v1.shape, jnp.int32)
    g = jnp.broadcast_to(v2[0:1], v1.shape)
    top = v1[0:1] + v2[0:1]
    z = jnp.zeros(top.shape, F32)
    for _ in range(k):
        f = v1 + g
        m = jnp.max(f, axis=0, keepdims=True)
        a = jnp.min(jnp.where(f == m, row, k), axis=0, keepdims=True)
        hit = row == a
        z = z + jnp.exp(m - top)
        p = jnp.where(hit, p + 1, p)
        nxt = jnp.max(jnp.where(hit, p, -1), axis=0, keepdims=True)
        v_nxt = jnp.max(jnp.where(row == nxt, v2, NEG_INF), axis=0, keepdims=True)
        g = jnp.where(hit, v_nxt, g)
    return p, z


def _peersel_kernel(q_ref, keys_ref, rank2_ref, cnt_ref, e1_ref, e2_ref):
    k = PEER_TOPK
    q = q_ref[...]
    s1 = _dot_nt(keys_ref[0, 0], q[:, :PEER_HALF])
    s2 = _dot_nt(keys_ref[0, 1], q[:, PEER_HALF:])
    v1, idx1, _ = _topk_rows(s1, k, False)
    v2, _, rank2 = _topk_rows(s2, k, True)
    picks, z = _top_pair_sums(v1, v2, k)
    key = lax.broadcasted_iota(jnp.int32, s1.shape, 0)
    cnt = jnp.zeros(s1.shape, jnp.int32)
    for r in range(k):
        cnt = jnp.where(key == idx1[r:r + 1], picks[r:r + 1], cnt)
    rank2_ref[0] = rank2.astype(rank2_ref.dtype)
    cnt_ref[0] = cnt.astype(cnt_ref.dtype)
    e1_ref[0] = (jnp.exp(s1 - v1[0:1]) / z).astype(e1_ref.dtype)
    e2_ref[0] = jnp.exp(s2 - v2[0:1]).astype(e2_ref.dtype)


def _peersel(q, sub_keys, tt):
    t = q.shape[0]
    spec = pl.BlockSpec((1, PEER_NKEYS, tt), lambda i, h: (h, 0, i))
    tile = jax.ShapeDtypeStruct((PEER_HEADS, PEER_NKEYS, t), MM)
    rows = jax.ShapeDtypeStruct((PEER_HEADS, PEER_NKEYS, t), F32)
    return pl.pallas_call(
        _peersel_kernel,
        grid=(t // tt, PEER_HEADS),
        in_specs=[pl.BlockSpec((tt, 2 * PEER_HALF), lambda i, h: (i, h)),
                  pl.BlockSpec((1, 2, PEER_NKEYS, PEER_HALF), lambda i, h: (h, 0, 0, 0))],
        out_specs=[spec] * 4, out_shape=[tile, rows, rows, tile],
        compiler_params=_cparams("parallel", "arbitrary"), name="peersel",
    )(q, sub_keys)


PEER_I1_PER_STEP = 4


def _peerffn_kernel(xn_ref, u_ref, v_ref, rank2_ref, cnt_ref, e1_ref, e2_ref, out_ref, acc_scr):
    j = pl.program_id(1)

    @pl.when(j == 0)
    def _zero():
        acc_scr[...] = jnp.zeros(acc_scr.shape, F32)

    hid = _dot_nt(u_ref[...], xn_ref[...])
    act = 0.5 * hid * (1.0 + lax.erf(hid * (2.0 ** -0.5)))
    tt = xn_ref.shape[0]
    wdt = rank2_ref.dtype
    pack = 4 // jnp.dtype(wdt).itemsize * SUBLANES
    grp = (PEER_NKEYS // pack, pack, tt)
    blocks = []
    for a in range(PEER_I1_PER_STEP):
        i1 = j * PEER_I1_PER_STEP + a
        w = jnp.zeros(grp, wdt)
        for h in range(PEER_HEADS):
            cnt = jnp.broadcast_to(cnt_ref[h, pl.ds(i1, 1), :], (pack, tt)).astype(wdt)
            e1 = jnp.broadcast_to(e1_ref[h, pl.ds(i1, 1), :], (pack, tt)).astype(wdt)
            w = w + jnp.where(rank2_ref[h].reshape(grp) < cnt, e2_ref[h].reshape(grp), jnp.zeros_like(w)) * e1
        blocks.append(w.reshape(PEER_NKEYS, tt) * act[a * PEER_NKEYS:(a + 1) * PEER_NKEYS].astype(wdt))
    coef = jnp.concatenate(blocks, axis=0)
    acc_scr[...] += _dot_tn(v_ref[...], coef)

    @pl.when(j == pl.num_programs(1) - 1)
    def _store():
        out_ref[...] = acc_scr[...].T


def _peerffn(xn, u, v, rank2, cnt, e1, e2, tt):
    t, d = xn.shape
    ne = PEER_I1_PER_STEP * PEER_NKEYS
    sel = pl.BlockSpec((PEER_HEADS, PEER_NKEYS, tt), lambda i, j: (0, 0, i))
    return pl.pallas_call(
        _peerffn_kernel,
        grid=(t // tt, PEER_NKEYS // PEER_I1_PER_STEP),
        in_specs=[pl.BlockSpec((tt, d), lambda i, j: (i, 0)),
                  pl.BlockSpec((ne, d), lambda i, j: (j, 0)), pl.BlockSpec((ne, d), lambda i, j: (j, 0)),
                  sel, sel, sel, sel],
        out_specs=pl.BlockSpec((tt, d), lambda i, j: (i, 0)),
        out_shape=jax.ShapeDtypeStruct((t, d), F32),
        scratch_shapes=[pltpu.VMEM((d, tt), F32)],
        compiler_params=_cparams("parallel", "arbitrary"), name="peerffn",
    )(xn, u, v, rank2, cnt, e1, e2)


def _ple_kernel(h_ref, peer_ref, p_ref, g_ref, wg_ref, wp_ref, out_ref):
    h2 = h_ref[...] + peer_ref[...]
    gate = jax.nn.sigmoid(_dot(_rms(h2, g_ref[...]), wg_ref[...]))
    out_ref[...] = h2 + gate * _dot(p_ref[...], wp_ref[...])


def _ple(h1, peer, p, g_ple, w_pg, w_pp, tt):
    t, d = h1.shape
    row = lambda i: (i, 0)
    const = lambda i: (0, 0)
    return pl.pallas_call(
        _ple_kernel,
        grid=(t // tt,),
        in_specs=[pl.BlockSpec((tt, d), row), pl.BlockSpec((tt, d), row), pl.BlockSpec((tt, p.shape[1]), row),
                  pl.BlockSpec((1, d), const), pl.BlockSpec(w_pg.shape, const), pl.BlockSpec(w_pp.shape, const)],
        out_specs=pl.BlockSpec((tt, d), row),
        out_shape=jax.ShapeDtypeStruct((t, d), F32),
        compiler_params=_cparams("parallel"), name="ple",
    )(h1, peer, p, g_ple, w_pg, w_pp)


def _token_tile(t, pref):
    return pref if t % pref == 0 else t


def kernel(x_prompt, x_sample, cache_moba_k, cache_moba_v, cache_diff_k, cache_diff_v, page_table, p_prompt, p_sample, rel_bias, g_attn, w_in, g_q_moba, g_k_moba, g_q_diff, g_k_diff, lam_q1, lam_k1, lam_q2, lam_k2, g_subln, w_out, g_ffn, w_peer_q, peer_sub_keys, peer_u, peer_v, g_ple, w_ple_gate, w_ple_proj):
    bsz, seq, d = x_prompt.shape
    nb, tn, _ = x_sample.shape
    depth = w_in.shape[0]
    page = cache_moba_k.shape[2]
    assert page == PAGE_ROWS and seq % MOBA_BLOCK == 0 and tn <= TOKEN_ROWS
    past_len = page_table.shape[1] * page
    assert past_len % (PAGES_PER_STEP * page) == 0 and past_len // MOBA_BLOCK <= LANES

    seg = jnp.asarray(np.kron(np.eye(MOBA_W // HEAD_DIM), np.full((HEAD_DIM, HEAD_DIM), 1.0 / HEAD_DIM)), jnp.bfloat16)
    caches = (jnp.transpose(cache_moba_k, (0, 1, 3, 4, 2)), jnp.transpose(cache_moba_v, (0, 1, 3, 4, 2)),
              jnp.transpose(cache_diff_k, (0, 1, 3, 4, 5, 2)),
              cache_diff_v.reshape(depth, cache_diff_v.shape[1], page * DIFF_HEADS, DIFF_VDIM))
    tile8 = lambda g: jnp.tile(g, MOBA_W // HEAD_DIM)

    hp = x_prompt.reshape(bsz * seq, d)
    hs = x_sample.reshape(nb * tn, d)
    new_p = [[] for _ in range(4)]
    new_s = [[] for _ in range(4)]
    w0, w1, w2, w3, w4, w5 = (slice(c * MOBA_W, (c + 1) * MOBA_W) for c in range(6))
    for i in range(depth):
        lam_init = 0.8 - 0.6 * math.exp(-0.3 * i)
        lam = (jnp.exp(jnp.sum(lam_q1[i] * lam_k1[i])) - jnp.exp(jnp.sum(lam_q2[i] * lam_k2[i])) + lam_init)
        lam = lam.reshape(1).astype(F32)
        wi = w_in[i]
        wkv = jnp.concatenate([wi[:, w1], wi[:, w4], wi[:, w2], wi[:, w5]], axis=1).astype(MM)
        wqt = jnp.concatenate([wi[:, w0], wi[:, w3]], axis=1).T.astype(MM)
        wvt = jnp.concatenate([wi[:, w2], wi[:, w5]], axis=1).T.astype(MM)
        gk = jnp.concatenate([tile8(g_k_moba[i]), tile8(g_k_diff[i])])[None, :]
        gqt = (jnp.concatenate([tile8(g_q_moba[i]), tile8(g_q_diff[i])]) * (ATTN_SCALE * LOG2E))[:, None]
        w_out_i = w_out[i].astype(MM)
        w_pq_i = w_peer_q[i].astype(MM)
        u_i = peer_u[i].astype(MM)
        v_i = peer_v[i].astype(MM)
        w_pg_i = w_ple_gate[i].astype(MM)
        w_pp_i = w_ple_proj[i].astype(MM)

        tp = _token_tile(bsz * seq, 512)
        kmt, vmt, kdt, vd, kb, qt, vt, kmean = _proj(hp, g_attn[i][None], wkv, wqt, wvt, seg, gk, gqt, tp, bsz, True)
        for lst, a in zip(new_p, (kmt, vmt, kdt, vd)):
            lst.append(a)
        kmean = kmean.reshape(bsz, seq // MOBA_BLOCK, MOBA_W)
        kmean = jnp.pad(kmean, ((0, 0), (0, -kmean.shape[1] % SUBLANES), (0, 0)))
        om_p, od_p = _prompt_attn(qt, kb, vt, kmean, rel_bias, lam, bsz)

        ts = _token_tile(nb * tn, 128)
        kmt, vmt, kdt, vd, _, qt, _ = _proj(hs, g_attn[i][None], wkv, wqt, wvt, seg, gk, gqt, ts, 1, False)
        for lst, a in zip(new_s, (kmt, vmt, kdt, vd)):
            lst.append(a)
        qs = qt.T.astype(F32).reshape(nb, 1, tn, QK_W)
        qs = jnp.pad(qs, ((0, 0), (0, 0), (0, TOKEN_ROWS - tn), (0, 0)))
        qs = jnp.broadcast_to(qs, (nb, MAPS_PER_TOKEN, TOKEN_ROWS, QK_W)).reshape(nb, SROWS, QK_W)
        om_s, od_s = _sample_attn(qs[:, :, :MOBA_W], qs[:, :, MOBA_W:], kmt[0], vmt[0], kdt[0], vd, caches,
                                  page_table, i, rel_bias, lam, tn)

        outs = []
        for h, om, od, p, tt in ((hp, om_p, od_p, p_prompt[i].reshape(bsz * seq, -1), tp),
                                 (hs, om_s.reshape(-1, MOBA_W), od_s.reshape(-1, DIFF_W),
                                  p_sample[i].reshape(nb * tn, -1), ts)):
            h1, xn, q = _merge(h, om, od, g_subln[i][None], w_out_i, g_ffn[i][None], w_pq_i, 1.0 - lam_init, tt)
            rank2, cnt, e1, e2 = _peersel(q, peer_sub_keys[i], tt)
            peer = _peerffn(xn, u_i, v_i, rank2, cnt, e1, e2, tt)
            outs.append(_ple(h1, peer, p, g_ple[i][None], w_pg_i, w_pp_i, tt))
        hp, hs = outs

    def stack(lst, lead, tail):
        def rows(a):
            if a.ndim == 3:
                a = jnp.moveaxis(a.reshape(a.shape[0], *tail, a.shape[2]), -1, 1)
            return a.reshape(*lead, *tail)
        return jnp.stack([rows(a) for a in lst])

    lp, ls = (bsz, seq), (nb, tn)
    tails = ((MOBA_HEADS, HEAD_DIM), (MOBA_HEADS, HEAD_DIM), (DIFF_HEADS, 2, HEAD_DIM), (DIFF_HEADS, DIFF_VDIM))
    return (hp.reshape(bsz, seq, d), hs.reshape(nb, tn, d),
            *[stack(l, lp, t) for l, t in zip(new_p, tails)],
            *[stack(l, ls, t) for l, t in zip(new_s, tails)])
```

```python
import functools
import math

import numpy as np
import jax
import jax.numpy as jnp
from jax import lax
from jax.experimental import pallas as pl
from jax.experimental.pallas import tpu as pltpu

F32 = jnp.float32
MM = jnp.bfloat16

HEAD_DIM = 64
MOBA_HEADS = 8
MOBA_BLOCK = 256
MOBA_TOPK = 3
DIFF_HEADS = 4
DIFF_VDIM = 2 * HEAD_DIM
MOBA_W = MOBA_HEADS * HEAD_DIM
DIFF_W = DIFF_HEADS * DIFF_VDIM
QK_W = MOBA_W + DIFF_W
N_MAPS = MOBA_HEADS + 2 * DIFF_HEADS
ATTN_SCALE = HEAD_DIM ** -0.5
LOG2E = math.log2(math.e)
REL_BUCKETS = 32
REL_MAX_EXACT = 16
REL_MAX_DIST = 128
PEER_HEADS = 8
PEER_NKEYS = 128
PEER_HALF = 128
PEER_TOPK = 16
RMS_EPS = 1e-6
LANES = 128
SUBLANES = 8
NEG_INF = float("-inf")
VMEM_LIMIT = 56 * 1024 * 1024


def _cparams(*sem):
    return pltpu.CompilerParams(dimension_semantics=sem, vmem_limit_bytes=VMEM_LIMIT)


def _rms(x, g):
    return x * lax.rsqrt(jnp.mean(x * x, axis=-1, keepdims=True) + RMS_EPS) * g


def _dot(a, b):
    return jnp.dot(a.astype(MM), b.astype(MM), preferred_element_type=F32)


def _dot_nt(a, b):
    return lax.dot_general(a.astype(MM), b.astype(MM), (((1,), (1,)), ((), ())), preferred_element_type=F32)


def _dot_tn(a, b):
    return lax.dot_general(a.astype(MM), b.astype(MM), (((0,), (0,)), ((), ())), preferred_element_type=F32)


def _seg_mean(sq, seg):
    hi = sq.astype(jnp.bfloat16)
    lo = (sq - hi.astype(F32)).astype(jnp.bfloat16)
    return (jnp.dot(hi, seg, preferred_element_type=F32) + jnp.dot(lo, seg, preferred_element_type=F32))


def _rel_bucket(rel):
    n = jnp.maximum(rel, 0)
    nf = jnp.maximum(n, 1).astype(F32)
    large = REL_MAX_EXACT + (jnp.log(nf / REL_MAX_EXACT) / math.log(REL_MAX_DIST / REL_MAX_EXACT)
                             * (REL_BUCKETS - REL_MAX_EXACT)).astype(jnp.int32)
    large = jnp.minimum(large, REL_BUCKETS - 1)
    return jnp.where(n < REL_MAX_EXACT, n, large)


def _bias_col(m):
    return m if m < MOBA_HEADS else MOBA_HEADS + (m - MOBA_HEADS) // 2


def _proj_kernel(x_ref, g_ref, wkv_ref, wqt_ref, wvt_ref, seg_ref, gk_ref, gqt_ref,
                 kmt_ref, vmt_ref, kdt_ref, vd_ref, kb_ref, qt_ref, vt_ref, *maybe_kmean):
    xn = _rms(x_ref[...], g_ref[...]).astype(MM)
    z = jnp.dot(xn, wkv_ref[...], preferred_element_type=F32)
    seg = seg_ref[...]
    gk = gk_ref[...]
    ks = []
    for c in range(2):
        zc = z[:, c * MOBA_W:(c + 1) * MOBA_W]
        ks.append(zc * lax.rsqrt(_seg_mean(zc * zc, seg) + RMS_EPS) * gk[:, c * MOBA_W:(c + 1) * MOBA_W])
    kmt_ref[0] = ks[0].T
    kdt_ref[0] = ks[1].T
    kb_ref[...] = jnp.concatenate(ks, axis=-1).astype(kb_ref.dtype)
    vmt_ref[0] = z[:, 2 * MOBA_W:3 * MOBA_W].T
    vd_ref[...] = z[:, 3 * MOBA_W:4 * MOBA_W]
    tt = xn.shape[0]
    zq = _dot_nt(wqt_ref[...], xn).reshape(QK_W // HEAD_DIM, HEAD_DIM, tt)
    ms = jnp.mean(zq * zq, axis=1, keepdims=True)
    qt = (zq * lax.rsqrt(ms + RMS_EPS)).reshape(QK_W, tt) * gqt_ref[...]
    qt_ref[...] = qt.astype(qt_ref.dtype)
    vt_ref[...] = _dot_nt(wvt_ref[...], xn).astype(vt_ref.dtype)
    if maybe_kmean:
        kmean_ref, = maybe_kmean
        for c in range(tt // MOBA_BLOCK):
            kmean_ref[0, c:c + 1, :] = jnp.mean(ks[0][c * MOBA_BLOCK:(c + 1) * MOBA_BLOCK], axis=0, keepdims=True)


def _proj(x2d, g_attn, wkv, wqt, wvt, seg, gk, gqt, tt, n_seq, with_kmean):
    t, d = x2d.shape
    nt = t // tt
    tps = nt // n_seq
    row = lambda i: (i, 0)
    col = lambda i: (0, i)
    const = lambda i: (0, 0)
    full = lambda a: pl.BlockSpec(a.shape, const)
    feat = jax.ShapeDtypeStruct((n_seq, MOBA_W, t // n_seq), F32)
    feat_spec = pl.BlockSpec((1, MOBA_W, tt), lambda i: (i // tps, 0, i % tps))
    out_shape = [feat, feat, feat, jax.ShapeDtypeStruct((t, MOBA_W), F32),
                 jax.ShapeDtypeStruct((t, QK_W), MM), jax.ShapeDtypeStruct((QK_W, t), MM),
                 jax.ShapeDtypeStruct((QK_W, t), MM)]
    out_specs = [feat_spec, feat_spec, feat_spec, pl.BlockSpec((tt, MOBA_W), row),
                 pl.BlockSpec((tt, QK_W), row), pl.BlockSpec((QK_W, tt), col), pl.BlockSpec((QK_W, tt), col)]
    if with_kmean:
        nb = tt // MOBA_BLOCK
        out_shape.append(jax.ShapeDtypeStruct((nt, nb, MOBA_W), F32))
        out_specs.append(pl.BlockSpec((1, nb, MOBA_W), lambda i: (i, 0, 0)))
    return pl.pallas_call(
        _proj_kernel,
        grid=(nt,),
        in_specs=[pl.BlockSpec((tt, d), row), full(g_attn), full(wkv), full(wqt), full(wvt), full(seg), full(gk),
                  full(gqt)],
        out_specs=out_specs, out_shape=out_shape,
        compiler_params=_cparams("parallel"), name="proj",
    )(x2d, g_attn, wkv, wqt, wvt, seg, gk, gqt)


def _top_select(gate, pos, axis, n_sel, n_ok):
    sel = jnp.zeros(gate.shape, F32)
    big = gate.shape[axis]
    for k in range(n_sel):
        m = jnp.max(gate, axis=axis, keepdims=True)
        idx = jnp.min(jnp.where(gate == m, pos, big), axis=axis, keepdims=True)
        hit = pos == idx
        sel = jnp.where(hit & (n_ok > k), 1.0, sel)
        gate = jnp.where(hit, NEG_INF, gate)
    return sel


def _attn_kernel(qi_ref, kj_ref, last_ref, bias_smem, lam_smem,
                 qt_ref, kb_ref, vt_ref, kmean_ref,
                 om_ref, od_ref,
                 bias_scr, qp_scr, sel_scr, m_scr, l_scr, accm_scr, accd_scr, s_scr):
    b = pl.program_id(0)
    p = pl.program_id(1)
    qi = qi_ref[p]
    kj = kj_ref[p]
    d = qi - kj
    tq = qt_ref.shape[1]
    tk = kb_ref.shape[0]

    @pl.when((b == 0) & (p == 0))
    def _build_bias():
        c = lax.broadcasted_iota(jnp.int32, (tk, tq), 0)
        r = lax.broadcasted_iota(jnp.int32, (tk, tq), 1)
        for dd in range(2):
            rel = dd * tq + r - c
            bucket = _rel_bucket(rel)
            for m in range(N_MAPS):
                bt = jnp.zeros((tk, tq), F32)
                for k in range(REL_BUCKETS):
                    bt = jnp.where(bucket == k, bias_smem[k, _bias_col(m)], bt)
                if dd == 0:
                    bt = jnp.where(rel >= 0, bt, NEG_INF)
                bias_scr[m, dd] = bt * LOG2E
        for m in range(N_MAPS):
            bias_scr[m, 2] = jnp.full((tk, tq), bias_smem[REL_BUCKETS - 1, _bias_col(m)], F32) * LOG2E

    @pl.when(d == 0)
    def _new_q_tile():
        half = lax.broadcasted_iota(jnp.int32, (2 * HEAD_DIM, tq), 0) // HEAD_DIM
        for j in range(N_MAPS // 2):
            qpair = qt_ref[j * 2 * HEAD_DIM:(j + 1) * 2 * HEAD_DIM, :].astype(F32)
            for e in range(2):
                qp_scr[2 * j + e] = jnp.where(half == e, qpair, 0.0).astype(qp_scr.dtype)
        blk = lax.broadcasted_iota(jnp.int32, (kmean_ref.shape[1], tq), 0)
        qblk = qi
        kmean = kmean_ref[0]
        for h in range(MOBA_HEADS):
            j = h // 2
            gate = _dot(kmean[:, j * 2 * HEAD_DIM:(j + 1) * 2 * HEAD_DIM], qp_scr[h])
            gate = jnp.where(blk < qblk, gate, NEG_INF)
            sel = _top_select(gate, blk, 0, MOBA_TOPK, qblk)
            sel_scr[h] = jnp.where(blk == qblk, 1.0, sel)
        m_scr[...] = jnp.full(m_scr.shape, NEG_INF, F32)
        l_scr[...] = jnp.zeros(l_scr.shape, F32)
        accm_scr[...] = jnp.zeros(accm_scr.shape, F32)
        accd_scr[...] = jnp.zeros(accd_scr.shape, F32)

    def scores(m):
        j = m // 2
        k_pair = kb_ref[:, j * 2 * HEAD_DIM:(j + 1) * 2 * HEAD_DIM]
        return jnp.dot(k_pair, qp_scr[m], preferred_element_type=F32)

    dsel = jnp.minimum(d, 2)
    ones = jnp.ones((2 * SUBLANES, tk), MM)
    n_buf = s_scr.shape[0]
    for m in range(n_buf - 1):
        s_scr[m] = scores(m)
    for m in range(N_MAPS):
        if m + n_buf - 1 < N_MAPS:
            s_scr[(m + n_buf - 1) % n_buf] = scores(m + n_buf - 1)
        s = s_scr[m % n_buf] + bias_scr[m, dsel]
        if m < MOBA_HEADS:
            s = jnp.where(sel_scr[m, pl.ds(kj, 1), :] > 0.5, s, NEG_INF)
            vt = vt_ref[m * HEAD_DIM:(m + 1) * HEAD_DIM, :]
        else:
            hv = (m - MOBA_HEADS) // 2
            vt = vt_ref[MOBA_W + hv * DIFF_VDIM:MOBA_W + (hv + 1) * DIFF_VDIM, :]
        m_prev = m_scr[m]
        m_new = jnp.maximum(m_prev, jnp.max(s, axis=0, keepdims=True))
        alpha = jnp.exp2(m_prev - m_new)
        pexp = jnp.exp2((s - m_new).astype(MM))
        m_scr[m] = m_new
        dv = vt.shape[0]
        pv = jnp.dot(jnp.concatenate([vt, ones], axis=0), pexp, preferred_element_type=F32)
        l_scr[m] = alpha * l_scr[m] + pv[dv:dv + 1]
        if m < MOBA_HEADS:
            accm_scr[m] = alpha * accm_scr[m] + pv[:dv]
        else:
            accd_scr[m - MOBA_HEADS] = alpha * accd_scr[m - MOBA_HEADS] + pv[:dv]

    @pl.when(last_ref[p] == 1)
    def _finish():
        lam = lam_smem[0]
        omt = jnp.concatenate([accm_scr[h] / l_scr[h] for h in range(MOBA_HEADS)], axis=0)
        om_ref[...] = omt.T
        outs = []
        for h in range(DIFF_HEADS):
            o0 = accd_scr[2 * h] / l_scr[MOBA_HEADS + 2 * h]
            o1 = accd_scr[2 * h + 1] / l_scr[MOBA_HEADS + 2 * h + 1]
            outs.append(o0 - lam * o1)
        od_ref[...] = jnp.concatenate(outs, axis=0).T


SCORE_BUFS = 5


def _attn_schedule(nq):
    qi, kj, last = [], [], []
    for i in range(nq):
        order = [i] + list(range(i))
        for n, j in enumerate(order):
            qi.append(i)
            kj.append(j)
            last.append(1 if n == len(order) - 1 else 0)
    return (jnp.asarray(qi, jnp.int32), jnp.asarray(kj, jnp.int32), jnp.asarray(last, jnp.int32))


def _prompt_attn(qt, kb, vt, kmean, rel_bias, lam, bsz):
    nbp = kmean.shape[1]
    s = kb.shape[0] // bsz
    tq = tk = MOBA_BLOCK
    nq = s // tq
    qi, kj, last = _attn_schedule(nq)
    qcol = pl.BlockSpec((QK_W, tq), lambda b, p, qi, kj, last: (0, b * nq + qi[p]))
    kcol = pl.BlockSpec((QK_W, tk), lambda b, p, qi, kj, last: (0, b * nq + kj[p]))
    krow = pl.BlockSpec((tk, QK_W), lambda b, p, qi, kj, last: (b * nq + kj[p], 0))
    orow = pl.BlockSpec((tq, MOBA_W), lambda b, p, qi, kj, last: (b * nq + qi[p], 0))
    smem = pl.BlockSpec(memory_space=pltpu.SMEM)
    grid_spec = pltpu.PrefetchScalarGridSpec(
        num_scalar_prefetch=3,
        grid=(bsz, int(qi.shape[0])),
        in_specs=[smem, smem, qcol, krow, kcol,
                  pl.BlockSpec((1, nbp, MOBA_W), lambda b, p, qi, kj, last: (b, 0, 0))],
        out_specs=[orow, orow],
        scratch_shapes=[
            pltpu.VMEM((N_MAPS, 3, tk, tq), F32),
            pltpu.VMEM((N_MAPS, 2 * HEAD_DIM, tq), MM),
            pltpu.VMEM((MOBA_HEADS, nbp, tq), F32),
            pltpu.VMEM((N_MAPS, 1, tq), F32),
            pltpu.VMEM((N_MAPS, 1, tq), F32),
            pltpu.VMEM((MOBA_HEADS, HEAD_DIM, tq), F32),
            pltpu.VMEM((2 * DIFF_HEADS, DIFF_VDIM, tq), F32),
            pltpu.VMEM((SCORE_BUFS, tk, tq), F32),
        ])
    t = kb.shape[0]
    return pl.pallas_call(
        _attn_kernel, grid_spec=grid_spec,
        out_shape=[jax.ShapeDtypeStruct((t, MOBA_W), F32), jax.ShapeDtypeStruct((t, DIFF_W), F32)],
        compiler_params=_cparams("arbitrary", "arbitrary"), name="attn",
    )(qi, kj, last, rel_bias, lam, qt, kb, vt, kmean)


PAGES_PER_STEP = 8
PAGE_ROWS = 128
TOKEN_ROWS = SUBLANES
MAPS_PER_TOKEN = 8
SROWS = MAPS_PER_TOKEN * TOKEN_ROWS


def _sattn_kernel(pt_ref, bias_smem, lam_smem,
                  qm_ref, qd_ref, kmn_ref, vmn_ref, kdn_ref, vdn_ref, *refs, tn):
    n = PAGES_PER_STEP
    mk_pages, mv_pages = refs[0:n], refs[n:2 * n]
    dk_pages, dv_pages = refs[2 * n:3 * n], refs[3 * n:4 * n]
    om_ref, od_ref = refs[4 * n], refs[4 * n + 1]
    qbm_scr, qbd_scr, gate_scr, sel_scr, m_scr, l_scr, far_scr, accm_scr, accd_scr = refs[4 * n + 2:]
    b = pl.program_id(0)
    ph = pl.program_id(1)
    j = pl.program_id(2)
    n_steps = pl.num_programs(2)
    tk = n * PAGE_ROWS
    past_len = n_steps * tk
    bps = tk // MOBA_BLOCK
    n_past = past_len // MOBA_BLOCK
    rows = SROWS

    row_m = lax.broadcasted_iota(jnp.int32, (rows, 1), 0) // TOKEN_ROWS
    row_t = lax.broadcasted_iota(jnp.int32, (rows, 1), 0) % TOKEN_ROWS
    moba_cols = [_bias_col(m) for m in range(MOBA_HEADS)]
    diff_cols = [_bias_col(MOBA_HEADS + m) for m in range(2 * DIFF_HEADS)]

    def bias_rows(rel, cols):
        bucket = _rel_bucket(rel)
        out = jnp.zeros(rel.shape, F32)
        for k in range(REL_BUCKETS):
            tab = jnp.zeros((rows, 1), F32)
            for mm, cc in enumerate(cols):
                tab = jnp.where(row_m == mm, bias_smem[k, cc], tab)
            out = jnp.where(bucket == k, tab, out)
        return out * LOG2E

    def page_bias(branch, rel, cols):
        return lax.cond(j == n_steps - 1, lambda: bias_rows(rel, cols),
                        lambda: jnp.broadcast_to(far_scr[branch], rel.shape))

    def kt_step(pages):
        return jnp.concatenate([r[0, 0].reshape(MOBA_W, PAGE_ROWS) for r in pages], axis=-1).astype(MM)

    @pl.when((ph == 0) & (j == 0))
    def _start():
        def keep_own_segment(q):
            lane_seg = lax.broadcasted_iota(jnp.int32, q.shape, 1) // HEAD_DIM
            return jnp.where(lane_seg == row_m, q, 0.0)
        qbm_scr[...] = keep_own_segment(qm_ref[0]).astype(qbm_scr.dtype)
        qbd_scr[...] = keep_own_segment(qd_ref[0]).astype(qbd_scr.dtype)
        gate_scr[...] = jnp.full(gate_scr.shape, NEG_INF, F32)
        far = jnp.full((rows, 1), REL_MAX_DIST, jnp.int32)
        far_scr[0] = bias_rows(far, moba_cols)
        far_scr[1] = bias_rows(far, diff_cols)

    @pl.when(ph == 0)
    def _gate():
        s = jnp.dot(qbm_scr[...], kt_step(mk_pages), preferred_element_type=F32)
        col = lax.broadcasted_iota(jnp.int32, (rows, LANES), 1)
        g = gate_scr[...]
        for c in range(bps):
            bs = jnp.sum(s[:, c * MOBA_BLOCK:(c + 1) * MOBA_BLOCK], axis=-1, keepdims=True)
            g = jnp.where(col == j * bps + c, bs, g)
        gate_scr[...] = g

        @pl.when(j == n_steps - 1)
        def _select():
            sel = _top_select(g, col, 1, min(MOBA_TOPK, n_past), jnp.full((rows, 1), n_past, jnp.int32))
            sel_scr[...] = sel.astype(sel_scr.dtype)
            m_scr[...] = jnp.full(m_scr.shape, NEG_INF, F32)
            l_scr[...] = jnp.zeros(l_scr.shape, F32)
            accm_scr[...] = jnp.zeros(accm_scr.shape, F32)
            accd_scr[...] = jnp.zeros(accd_scr.shape, F32)

    def softmax_step(branch, s):
        m_prev = m_scr[branch]
        m_new = jnp.maximum(m_prev, jnp.max(s, axis=-1, keepdims=True))
        m_safe = jnp.where(m_new == NEG_INF, 0.0, m_new)
        alpha = jnp.exp2(m_prev - m_safe)
        pexp = jnp.exp2(s - m_safe)
        l_scr[branch] = alpha * l_scr[branch] + jnp.sum(pexp, axis=-1, keepdims=True)
        m_scr[branch] = m_new
        return alpha, pexp.astype(MM)

    def diff_pv(alpha, pexp, v_of_head):
        for h in range(DIFF_HEADS):
            rs = slice(h * 2 * TOKEN_ROWS, (h + 1) * 2 * TOKEN_ROWS)
            accd_scr[rs] = alpha[rs] * accd_scr[rs] + jnp.dot(pexp[rs], v_of_head(h), preferred_element_type=F32)

    @pl.when(ph == 1)
    def _attend():
        kpos = j * tk + lax.broadcasted_iota(jnp.int32, (rows, tk), 1)
        rel = past_len + row_t - kpos
        blk_of_col = (lax.broadcasted_iota(jnp.int32, (LANES, tk), 1) // MOBA_BLOCK) + j * bps
        onehot = (lax.broadcasted_iota(jnp.int32, (LANES, tk), 0) == blk_of_col).astype(sel_scr.dtype)
        allowed = jnp.dot(sel_scr[...], onehot, preferred_element_type=F32)
        s_m = jnp.dot(qbm_scr[...], kt_step(mk_pages), preferred_element_type=F32) + page_bias(0, rel, moba_cols)
        alpha, pexp = softmax_step(0, jnp.where(allowed > 0.5, s_m, NEG_INF))
        accm_scr[...] = alpha * accm_scr[...] + _dot_nt(pexp, kt_step(mv_pages))
        s_d = jnp.dot(qbd_scr[...], kt_step(dk_pages), preferred_element_type=F32) + page_bias(1, rel, diff_cols)
        alpha, pexp = softmax_step(1, s_d)
        diff_pv(alpha, pexp, lambda h: jnp.concatenate(
            [r[0, 0, pl.ds(h, PAGE_ROWS, stride=DIFF_HEADS), :] for r in dv_pages], axis=0).astype(MM))

        @pl.when(j == n_steps - 1)
        def _finish():
            ncol = kmn_ref.shape[1]
            col = lax.broadcasted_iota(jnp.int32, (rows, ncol), 1)
            reln = row_t - (col - b * tn)
            ok = (col >= b * tn) & (col < (b + 1) * tn) & (reln >= 0)
            s_mn = _dot(qbm_scr[...], kmn_ref[...]) + bias_rows(reln, moba_cols)
            alpha, pexp = softmax_step(0, jnp.where(ok, s_mn, NEG_INF))
            accm = alpha * accm_scr[...] + _dot_nt(pexp, vmn_ref[...])
            s_dn = _dot(qbd_scr[...], kdn_ref[...]) + bias_rows(reln, diff_cols)
            alpha, pexp = softmax_step(1, jnp.where(ok, s_dn, NEG_INF))
            diff_pv(alpha, pexp, lambda h: vdn_ref[:, h * DIFF_VDIM:(h + 1) * DIFF_VDIM].astype(MM))
            lam = lam_smem[0]
            om = accm / l_scr[0]
            for h in range(MOBA_HEADS):
                om_ref[0, :, h * HEAD_DIM:(h + 1) * HEAD_DIM] = (
                    om[h * TOKEN_ROWS:h * TOKEN_ROWS + tn, h * HEAD_DIM:(h + 1) * HEAD_DIM])
            od = accd_scr[...] / l_scr[1]
            for h in range(DIFF_HEADS):
                o0 = od[2 * h * TOKEN_ROWS:2 * h * TOKEN_ROWS + tn]
                o1 = od[(2 * h + 1) * TOKEN_ROWS:(2 * h + 1) * TOKEN_ROWS + tn]
                od_ref[0, :, h * DIFF_VDIM:(h + 1) * DIFF_VDIM] = o0 - lam * o1


def _sample_attn(qm, qd, kmn, vmn, kdn, vdn, caches, page_table, layer, rel_bias, lam, tn):
    nb = qm.shape[0]
    n_pages = page_table.shape[1]
    n_steps = n_pages // PAGES_PER_STEP
    ncol = kmn.shape[1]
    qspec = pl.BlockSpec((1, SROWS, MOBA_W), lambda b, ph, j, pt: (b, 0, 0))
    newt = pl.BlockSpec((MOBA_W, ncol), lambda b, ph, j, pt: (0, 0))
    newr = pl.BlockSpec((ncol, DIFF_W), lambda b, ph, j, pt: (0, 0))
    out = pl.BlockSpec((1, tn, MOBA_W), lambda b, ph, j, pt: (b, 0, 0))
    smem = pl.BlockSpec(memory_space=pltpu.SMEM)

    def page_specs(cache, both_phases):
        zeros = (0,) * (cache.ndim - 2)

        def index_map(b, ph, j, pt, g):
            step = j if both_phases else j * ph
            return (layer, pt[b, step * PAGES_PER_STEP + g]) + zeros
        return [pl.BlockSpec((1, 1) + cache.shape[2:], functools.partial(index_map, g=g))
                for g in range(PAGES_PER_STEP)]

    mk, mv, dk, dv = caches
    grid_spec = pltpu.PrefetchScalarGridSpec(
        num_scalar_prefetch=1, grid=(nb, 2, n_steps),
        in_specs=([smem, smem, qspec, qspec, newt, newt, newt, newr]
                  + page_specs(mk, True) + page_specs(mv, False) + page_specs(dk, False) + page_specs(dv, False)),
        out_specs=[out, out],
        scratch_shapes=[
            pltpu.VMEM((SROWS, MOBA_W), MM), pltpu.VMEM((SROWS, MOBA_W), MM),
            pltpu.VMEM((SROWS, LANES), F32), pltpu.VMEM((SROWS, LANES), MM),
            pltpu.VMEM((2, SROWS, 1), F32), pltpu.VMEM((2, SROWS, 1), F32), pltpu.VMEM((2, SROWS, 1), F32),
            pltpu.VMEM((SROWS, MOBA_W), F32), pltpu.VMEM((SROWS, DIFF_VDIM), F32),
        ])
    args = []
    for c in caches:
        args += [c] * PAGES_PER_STEP
    return pl.pallas_call(
        functools.partial(_sattn_kernel, tn=tn), grid_spec=grid_spec,
        out_shape=[jax.ShapeDtypeStruct((nb, tn, MOBA_W), F32), jax.ShapeDtypeStruct((nb, tn, DIFF_W), F32)],
        compiler_params=_cparams("arbitrary", "arbitrary", "arbitrary"), name="sattn",
    )(page_table, rel_bias, lam, qm, qd, kmn, vmn, kdn, vdn, *args)


def _merge_kernel(sub_scale, h_ref, om_ref, od_ref, gsub_ref, wout_ref, gffn_ref, wpq_ref, h1_ref, xn_ref, q_ref):
    od = od_ref[...]
    parts = [om_ref[...]]
    for h in range(DIFF_HEADS):
        parts.append(_rms(od[:, h * DIFF_VDIM:(h + 1) * DIFF_VDIM], gsub_ref[...]) * sub_scale)
    y = jnp.concatenate(parts, axis=-1)
    h1 = h_ref[...] + _dot(y, wout_ref[...])
    h1_ref[...] = h1
    xn = _rms(h1, gffn_ref[...])
    xn_ref[...] = xn.astype(xn_ref.dtype)
    q_ref[...] = _dot(xn, wpq_ref[...]).astype(q_ref.dtype)


def _merge(h, om, od, g_sub, w_out, g_ffn, w_pq, sub_scale, tt):
    t, d = h.shape
    nq = w_pq.shape[1]
    row = lambda i: (i, 0)
    const = lambda i: (0, 0)
    return pl.pallas_call(
        functools.partial(_merge_kernel, sub_scale),
        grid=(t // tt,),
        in_specs=[pl.BlockSpec((tt, d), row), pl.BlockSpec((tt, MOBA_W), row), pl.BlockSpec((tt, DIFF_W), row),
                  pl.BlockSpec((1, DIFF_VDIM), const), pl.BlockSpec(w_out.shape, const),
                  pl.BlockSpec((1, d), const), pl.BlockSpec(w_pq.shape, const)],
        out_specs=[pl.BlockSpec((tt, d), row), pl.BlockSpec((tt, d), row), pl.BlockSpec((tt, nq), row)],
        out_shape=[jax.ShapeDtypeStruct((t, d), F32), jax.ShapeDtypeStruct((t, d), MM),
                   jax.ShapeDtypeStruct((t, nq), MM)],
        compiler_params=_cparams("parallel"), name="merge",
    )(h, om, od, g_sub, w_out, g_ffn, w_pq)


def _topk_rows(s, k, want_rank):
    n = s.shape[0]
    row = lax.broadcasted_iota(jnp.int32, s.shape, 0)
    rank = jnp.full(s.shape, k, jnp.int32) if want_rank else None
    vals = jnp.zeros((k, s.shape[1]), F32)
    idxs = jnp.zeros((k, s.shape[1]), jnp.int32)
    vrow = lax.broadcasted_iota(jnp.int32, vals.shape, 0)
    for r in range(k):
        m = jnp.max(s, axis=0, keepdims=True)
        idx = jnp.min(jnp.where(s == m, row, n), axis=0, keepdims=True)
        hit = row == idx
        if want_rank:
            rank = jnp.where(hit, r, rank)
        s = jnp.where(hit, NEG_INF, s)
        vals = jnp.where(vrow == r, m, vals)
        idxs = jnp.where(vrow == r, idx, idxs)
    return vals, idxs, rank


def _top_pair_sums(v1, v2, k):
    row = lax.broadcasted_iota(jnp.int32, v1.shape, 0)
    p = jnp.zeros(v1.shape, jnp.int32)
    g = jnp.broadcast_to(v2[0:1], v1.shape)
    top = v1[0:1] + v2[0:1]
    z = jnp.zeros(top.shape, F32)
    for _ in range(k):
        f = v1 + g
        m = jnp.max(f, axis=0, keepdims=True)
        a = jnp.min(jnp.where(f == m, row, k), axis=0, keepdims=True)
        hit = row == a
        z = z + jnp.exp(m - top)
        p = jnp.where(hit, p + 1, p)
        nxt = jnp.max(jnp.where(hit, p, -1), axis=0, keepdims=True)
        v_nxt = jnp.max(jnp.where(row == nxt, v2, NEG_INF), axis=0, keepdims=True)
        g = jnp.where(hit, v_nxt, g)
    return p, z


def _peersel_kernel(q_ref, keys_ref, rank2_ref, cnt_ref, e1_ref, e2_ref):
    k = PEER_TOPK
    q = q_ref[...]
    s1 = _dot_nt(keys_ref[0, 0], q[:, :PEER_HALF])
    s2 = _dot_nt(keys_ref[0, 1], q[:, PEER_HALF:])
    v1, idx1, _ = _topk_rows(s1, k, False)
    v2, _, rank2 = _topk_rows(s2, k, True)
    picks, z = _top_pair_sums(v1, v2, k)
    key = lax.broadcasted_iota(jnp.int32, s1.shape, 0)
    cnt = jnp.zeros(s1.shape, jnp.int32)
    for r in range(k):
        cnt = jnp.where(key == idx1[r:r + 1], picks[r:r + 1], cnt)
    rank2_ref[0] = rank2.astype(rank2_ref.dtype)
    cnt_ref[0] = cnt.astype(cnt_ref.dtype)
    e1_ref[0] = (jnp.exp(s1 - v1[0:1]) / z).astype(e1_ref.dtype)
    e2_ref[0] = jnp.exp(s2 - v2[0:1]).astype(e2_ref.dtype)


def _peersel(q, sub_keys, tt):
    t = q.shape[0]
    spec = pl.BlockSpec((1, PEER_NKEYS, tt), lambda i, h: (h, 0, i))
    tile = jax.ShapeDtypeStruct((PEER_HEADS, PEER_NKEYS, t), MM)
    rows = jax.ShapeDtypeStruct((PEER_HEADS, PEER_NKEYS, t), F32)
    return pl.pallas_call(
        _peersel_kernel,
        grid=(t // tt, PEER_HEADS),
        in_specs=[pl.BlockSpec((tt, 2 * PEER_HALF), lambda i, h: (i, h)),
                  pl.BlockSpec((1, 2, PEER_NKEYS, PEER_HALF), lambda i, h: (h, 0, 0, 0))],
        out_specs=[spec] * 4, out_shape=[tile, rows, rows, tile],
        compiler_params=_cparams("parallel", "arbitrary"), name="peersel",
    )(q, sub_keys)


PEER_I1_PER_STEP = SUBLANES
PEER_TOKEN_SPLIT = 2
PEER_TOKEN_TILE = 512


def _peerffn_kernel(xn_ref, u_ref, vt_ref, rank2_ref, cnt_ref, e1_ref, e2_ref, out_ref, hid_scr, coef_scr, acc_scr):
    n_split = hid_scr.shape[0]
    j = pl.program_id(1)

    @pl.when(j == 0)
    def _zero():
        acc_scr[...] = jnp.zeros(acc_scr.shape, F32)

    tt = xn_ref.shape[0]
    th = tt // n_split
    wdt = rank2_ref.dtype
    pack = 4 // jnp.dtype(wdt).itemsize * SUBLANES
    grp = (PEER_NKEYS // pack, pack, th)

    def hidden(hf):
        hid_scr[hf] = _dot_nt(u_ref[...], xn_ref[hf * th:(hf + 1) * th, :])

    def weights(hf):
        ts = slice(hf * th, (hf + 1) * th)
        for a in range(PEER_I1_PER_STEP):
            rs = slice(a * PEER_NKEYS, (a + 1) * PEER_NKEYS)
            hid = hid_scr[hf, rs]
            act = 0.5 * hid * (1.0 + lax.erf(hid * (2.0 ** -0.5)))
            w = jnp.zeros(grp, wdt)
            for h in range(PEER_HEADS):
                cnt = jnp.broadcast_to(cnt_ref[h, a:a + 1, ts], (pack, th)).astype(wdt)
                e1 = jnp.broadcast_to(e1_ref[h, a:a + 1, ts], (pack, th)).astype(wdt)
                w = w + jnp.where(rank2_ref[h, :, ts].reshape(grp) < cnt, e2_ref[h, :, ts].reshape(grp),
                                  jnp.zeros_like(w)) * e1
            coef_scr[hf, rs] = w.reshape(PEER_NKEYS, th) * act.astype(wdt)

    def project(hf):
        ts = slice(hf * th, (hf + 1) * th)
        acc_scr[:, ts] += jnp.dot(vt_ref[...], coef_scr[hf], preferred_element_type=F32)

    hidden(0)
    for hf in range(n_split):
        if hf + 1 < n_split:
            hidden(hf + 1)
        weights(hf)
        project(hf)

    @pl.when(j == pl.num_programs(1) - 1)
    def _store():
        out_ref[...] = acc_scr[...].T


def _peerffn(xn, u, vt, rank2, cnt, e1, e2, tt):
    t, d = xn.shape
    ne = PEER_I1_PER_STEP * PEER_NKEYS
    n_split = PEER_TOKEN_SPLIT if tt % (PEER_TOKEN_SPLIT * LANES) == 0 else 1
    th = tt // n_split
    tile = pl.BlockSpec((PEER_HEADS, PEER_NKEYS, tt), lambda i, j: (0, 0, i))
    rows = pl.BlockSpec((PEER_HEADS, PEER_I1_PER_STEP, tt), lambda i, j: (0, j, i))
    return pl.pallas_call(
        _peerffn_kernel,
        grid=(t // tt, PEER_NKEYS // PEER_I1_PER_STEP),
        in_specs=[pl.BlockSpec((tt, d), lambda i, j: (i, 0)),
                  pl.BlockSpec((ne, d), lambda i, j: (j, 0)), pl.BlockSpec((d, ne), lambda i, j: (0, j)),
                  tile, rows, rows, tile],
        out_specs=pl.BlockSpec((tt, d), lambda i, j: (i, 0)),
        out_shape=jax.ShapeDtypeStruct((t, d), F32),
        scratch_shapes=[pltpu.VMEM((n_split, ne, th), F32), pltpu.VMEM((n_split, ne, th), rank2.dtype),
                        pltpu.VMEM((d, tt), F32)],
        compiler_params=_cparams("parallel", "arbitrary"), name="peerffn",
    )(xn, u, vt, rank2, cnt, e1, e2)


def _ple_kernel(h_ref, peer_ref, p_ref, g_ref, wg_ref, wp_ref, out_ref):
    h2 = h_ref[...] + peer_ref[...]
    gate = jax.nn.sigmoid(_dot(_rms(h2, g_ref[...]), wg_ref[...]))
    out_ref[...] = h2 + gate * _dot(p_ref[...], wp_ref[...])


def _ple(h1, peer, p, g_ple, w_pg, w_pp, tt):
    t, d = h1.shape
    row = lambda i: (i, 0)
    const = lambda i: (0, 0)
    return pl.pallas_call(
        _ple_kernel,
        grid=(t // tt,),
        in_specs=[pl.BlockSpec((tt, d), row), pl.BlockSpec((tt, d), row), pl.BlockSpec((tt, p.shape[1]), row),
                  pl.BlockSpec((1, d), const), pl.BlockSpec(w_pg.shape, const), pl.BlockSpec(w_pp.shape, const)],
        out_specs=pl.BlockSpec((tt, d), row),
        out_shape=jax.ShapeDtypeStruct((t, d), F32),
        compiler_params=_cparams("parallel"), name="ple",
    )(h1, peer, p, g_ple, w_pg, w_pp)


def _token_tile(t, pref):
    return pref if t % pref == 0 else t


def kernel(x_prompt, x_sample, cache_moba_k, cache_moba_v, cache_diff_k, cache_diff_v, page_table, p_prompt, p_sample, rel_bias, g_attn, w_in, g_q_moba, g_k_moba, g_q_diff, g_k_diff, lam_q1, lam_k1, lam_q2, lam_k2, g_subln, w_out, g_ffn, w_peer_q, peer_sub_keys, peer_u, peer_v, g_ple, w_ple_gate, w_ple_proj):
    bsz, seq, d = x_prompt.shape
    nb, tn, _ = x_sample.shape
    depth = w_in.shape[0]
    page = cache_moba_k.shape[2]
    assert page == PAGE_ROWS and seq % MOBA_BLOCK == 0 and tn <= TOKEN_ROWS
    past_len = page_table.shape[1] * page
    assert past_len % (PAGES_PER_STEP * page) == 0 and past_len // MOBA_BLOCK <= LANES

    seg = jnp.asarray(np.kron(np.eye(MOBA_W // HEAD_DIM), np.full((HEAD_DIM, HEAD_DIM), 1.0 / HEAD_DIM)), jnp.bfloat16)
    caches = (jnp.transpose(cache_moba_k, (0, 1, 3, 4, 2)), jnp.transpose(cache_moba_v, (0, 1, 3, 4, 2)),
              jnp.transpose(cache_diff_k, (0, 1, 3, 4, 5, 2)),
              cache_diff_v.reshape(depth, cache_diff_v.shape[1], page * DIFF_HEADS, DIFF_VDIM))
    tile8 = lambda g: jnp.tile(g, MOBA_W // HEAD_DIM)

    hp = x_prompt.reshape(bsz * seq, d)
    hs = x_sample.reshape(nb * tn, d)
    new_p = [[] for _ in range(4)]
    new_s = [[] for _ in range(4)]
    w0, w1, w2, w3, w4, w5 = (slice(c * MOBA_W, (c + 1) * MOBA_W) for c in range(6))
    for i in range(depth):
        lam_init = 0.8 - 0.6 * math.exp(-0.3 * i)
        lam = (jnp.exp(jnp.sum(lam_q1[i] * lam_k1[i])) - jnp.exp(jnp.sum(lam_q2[i] * lam_k2[i])) + lam_init)
        lam = lam.reshape(1).astype(F32)
        wi = w_in[i]
        wkv = jnp.concatenate([wi[:, w1], wi[:, w4], wi[:, w2], wi[:, w5]], axis=1).astype(MM)
        wqt = jnp.concatenate([wi[:, w0], wi[:, w3]], axis=1).T.astype(MM)
        wvt = jnp.concatenate([wi[:, w2], wi[:, w5]], axis=1).T.astype(MM)
        gk = jnp.concatenate([tile8(g_k_moba[i]), tile8(g_k_diff[i])])[None, :]
        gqt = (jnp.concatenate([tile8(g_q_moba[i]), tile8(g_q_diff[i])]) * (ATTN_SCALE * LOG2E))[:, None]
        w_out_i = w_out[i].astype(MM)
        w_pq_i = w_peer_q[i].astype(MM)
        u_i = peer_u[i].astype(MM)
        vt_i = peer_v[i].T.astype(MM)
        w_pg_i = w_ple_gate[i].astype(MM)
        w_pp_i = w_ple_proj[i].astype(MM)

        tp = _token_tile(bsz * seq, 512)
        kmt, vmt, kdt, vd, kb, qt, vt, kmean = _proj(hp, g_attn[i][None], wkv, wqt, wvt, seg, gk, gqt, tp, bsz, True)
        for lst, a in zip(new_p, (kmt, vmt, kdt, vd)):
            lst.append(a)
        kmean = kmean.reshape(bsz, seq // MOBA_BLOCK, MOBA_W)
        kmean = jnp.pad(kmean, ((0, 0), (0, -kmean.shape[1] % SUBLANES), (0, 0)))
        om_p, od_p = _prompt_attn(qt, kb, vt, kmean, rel_bias, lam, bsz)

        ts = _token_tile(nb * tn, 128)
        kmt, vmt, kdt, vd, _, qt, _ = _proj(hs, g_attn[i][None], wkv, wqt, wvt, seg, gk, gqt, ts, 1, False)
        for lst, a in zip(new_s, (kmt, vmt, kdt, vd)):
            lst.append(a)
        qs = qt.T.astype(F32).reshape(nb, 1, tn, QK_W)
        qs = jnp.pad(qs, ((0, 0), (0, 0), (0, TOKEN_ROWS - tn), (0, 0)))
        qs = jnp.broadcast_to(qs, (nb, MAPS_PER_TOKEN, TOKEN_ROWS, QK_W)).reshape(nb, SROWS, QK_W)
        om_s, od_s = _sample_attn(qs[:, :, :MOBA_W], qs[:, :, MOBA_W:], kmt[0], vmt[0], kdt[0], vd, caches,
                                  page_table, i, rel_bias, lam, tn)

        outs = []
        for h, om, od, p, tt in ((hp, om_p, od_p, p_prompt[i].reshape(bsz * seq, -1), tp),
                                 (hs, om_s.reshape(-1, MOBA_W), od_s.reshape(-1, DIFF_W),
                                  p_sample[i].reshape(nb * tn, -1), ts)):
            h1, xn, q = _merge(h, om, od, g_subln[i][None], w_out_i, g_ffn[i][None], w_pq_i, 1.0 - lam_init, tt)
            rank2, cnt, e1, e2 = _peersel(q, peer_sub_keys[i], tt)
            peer = _peerffn(xn, u_i, vt_i, rank2, cnt, e1, e2, _token_tile(xn.shape[0], PEER_TOKEN_TILE))
            outs.append(_ple(h1, peer, p, g_ple[i][None], w_pg_i, w_pp_i, tt))
        hp, hs = outs

    def stack(lst, lead, tail):
        def rows(a):
            if a.ndim == 3:
                a = jnp.moveaxis(a.reshape(a.shape[0], *tail, a.shape[2]), -1, 1)
            return a.reshape(*lead, *tail)
        return jnp.stack([rows(a) for a in lst])

    lp, ls = (bsz, seq), (nb, tn)
    tails = ((MOBA_HEADS, HEAD_DIM), (MOBA_HEADS, HEAD_DIM), (DIFF_HEADS, 2, HEAD_DIM), (DIFF_HEADS, DIFF_VDIM))
    return (hp.reshape(bsz, seq, d), hs.reshape(nb, tn, d),
            *[stack(l, lp, t) for l, t in zip(new_p, tails)],
            *[stack(l, ls, t) for l, t in zip(new_s, tails)])
```

```python
import functools
import math

import numpy as np
import jax
import jax.numpy as jnp
from jax import lax
from jax.experimental import pallas as pl
from jax.experimental.pallas import tpu as pltpu

F32 = jnp.float32
MM = jnp.bfloat16

HEAD_DIM = 64
MOBA_HEADS = 8
MOBA_BLOCK = 256
MOBA_TOPK = 3
DIFF_HEADS = 4
DIFF_VDIM = 2 * HEAD_DIM
MOBA_W = MOBA_HEADS * HEAD_DIM
DIFF_W = DIFF_HEADS * DIFF_VDIM
QK_W = MOBA_W + DIFF_W
N_MAPS = MOBA_HEADS + 2 * DIFF_HEADS
ATTN_SCALE = HEAD_DIM ** -0.5
LOG2E = math.log2(math.e)
REL_BUCKETS = 32
REL_MAX_EXACT = 16
REL_MAX_DIST = 128
PEER_HEADS = 8
PEER_NKEYS = 128
PEER_HALF = 128
PEER_TOPK = 16
RMS_EPS = 1e-6
LANES = 128
SUBLANES = 8
NEG_INF = float("-inf")
VMEM_LIMIT = 56 * 1024 * 1024


def _cparams(*sem):
    return pltpu.CompilerParams(dimension_semantics=sem, vmem_limit_bytes=VMEM_LIMIT)


def _rms(x, g):
    return x * lax.rsqrt(jnp.mean(x * x, axis=-1, keepdims=True) + RMS_EPS) * g


def _dot(a, b):
    return jnp.dot(a.astype(MM), b.astype(MM), preferred_element_type=F32)


def _dot_nt(a, b):
    return lax.dot_general(a.astype(MM), b.astype(MM), (((1,), (1,)), ((), ())), preferred_element_type=F32)


def _dot_tn(a, b):
    return lax.dot_general(a.astype(MM), b.astype(MM), (((0,), (0,)), ((), ())), preferred_element_type=F32)


def _seg_mean(sq, seg):
    hi = sq.astype(jnp.bfloat16)
    lo = (sq - hi.astype(F32)).astype(jnp.bfloat16)
    return (jnp.dot(hi, seg, preferred_element_type=F32) + jnp.dot(lo, seg, preferred_element_type=F32))


def _rel_bucket(rel):
    n = jnp.maximum(rel, 0)
    nf = jnp.maximum(n, 1).astype(F32)
    large = REL_MAX_EXACT + (jnp.log(nf / REL_MAX_EXACT) / math.log(REL_MAX_DIST / REL_MAX_EXACT)
                             * (REL_BUCKETS - REL_MAX_EXACT)).astype(jnp.int32)
    large = jnp.minimum(large, REL_BUCKETS - 1)
    return jnp.where(n < REL_MAX_EXACT, n, large)


def _bias_col(m):
    return m if m < MOBA_HEADS else MOBA_HEADS + (m - MOBA_HEADS) // 2


def _proj_kernel(x_ref, g_ref, wkv_ref, wqt_ref, wvt_ref, seg_ref, gk_ref, gqt_ref,
                 kmt_ref, vmt_ref, kdt_ref, vd_ref, kb_ref, qt_ref, vt_ref, *maybe_kmean):
    xn = _rms(x_ref[...], g_ref[...]).astype(MM)
    z = jnp.dot(xn, wkv_ref[...], preferred_element_type=F32)
    seg = seg_ref[...]
    gk = gk_ref[...]
    ks = []
    for c in range(2):
        zc = z[:, c * MOBA_W:(c + 1) * MOBA_W]
        ks.append(zc * lax.rsqrt(_seg_mean(zc * zc, seg) + RMS_EPS) * gk[:, c * MOBA_W:(c + 1) * MOBA_W])
    kmt_ref[0] = ks[0].T
    kdt_ref[0] = ks[1].T
    kb_ref[...] = jnp.concatenate(ks, axis=-1).astype(kb_ref.dtype)
    vmt_ref[0] = z[:, 2 * MOBA_W:3 * MOBA_W].T
    vd_ref[...] = z[:, 3 * MOBA_W:4 * MOBA_W]
    tt = xn.shape[0]
    zq = _dot_nt(wqt_ref[...], xn).reshape(QK_W // HEAD_DIM, HEAD_DIM, tt)
    ms = jnp.mean(zq * zq, axis=1, keepdims=True)
    qt = (zq * lax.rsqrt(ms + RMS_EPS)).reshape(QK_W, tt) * gqt_ref[...]
    qt_ref[...] = qt.astype(qt_ref.dtype)
    vt_ref[...] = _dot_nt(wvt_ref[...], xn).astype(vt_ref.dtype)
    if maybe_kmean:
        kmean_ref, = maybe_kmean
        for c in range(tt // MOBA_BLOCK):
            kmean_ref[0, c:c + 1, :] = jnp.mean(ks[0][c * MOBA_BLOCK:(c + 1) * MOBA_BLOCK], axis=0, keepdims=True)


def _proj(x2d, g_attn, wkv, wqt, wvt, seg, gk, gqt, tt, n_seq, with_kmean):
    t, d = x2d.shape
    nt = t // tt
    tps = nt // n_seq
    row = lambda i: (i, 0)
    col = lambda i: (0, i)
    const = lambda i: (0, 0)
    full = lambda a: pl.BlockSpec(a.shape, const)
    feat = jax.ShapeDtypeStruct((n_seq, MOBA_W, t // n_seq), F32)
    feat_spec = pl.BlockSpec((1, MOBA_W, tt), lambda i: (i // tps, 0, i % tps))
    out_shape = [feat, feat, feat, jax.ShapeDtypeStruct((t, MOBA_W), F32),
                 jax.ShapeDtypeStruct((t, QK_W), MM), jax.ShapeDtypeStruct((QK_W, t), MM),
                 jax.ShapeDtypeStruct((QK_W, t), MM)]
    out_specs = [feat_spec, feat_spec, feat_spec, pl.BlockSpec((tt, MOBA_W), row),
                 pl.BlockSpec((tt, QK_W), row), pl.BlockSpec((QK_W, tt), col), pl.BlockSpec((QK_W, tt), col)]
    if with_kmean:
        nb = tt // MOBA_BLOCK
        out_shape.append(jax.ShapeDtypeStruct((nt, nb, MOBA_W), F32))
        out_specs.append(pl.BlockSpec((1, nb, MOBA_W), lambda i: (i, 0, 0)))
    return pl.pallas_call(
        _proj_kernel,
        grid=(nt,),
        in_specs=[pl.BlockSpec((tt, d), row), full(g_attn), full(wkv), full(wqt), full(wvt), full(seg), full(gk),
                  full(gqt)],
        out_specs=out_specs, out_shape=out_shape,
        compiler_params=_cparams("parallel"), name="proj",
    )(x2d, g_attn, wkv, wqt, wvt, seg, gk, gqt)


def _top_select(gate, pos, axis, n_sel, n_ok):
    sel = jnp.zeros(gate.shape, F32)
    big = gate.shape[axis]
    for k in range(n_sel):
        m = jnp.max(gate, axis=axis, keepdims=True)
        idx = jnp.min(jnp.where(gate == m, pos, big), axis=axis, keepdims=True)
        hit = pos == idx
        sel = jnp.where(hit & (n_ok > k), 1.0, sel)
        gate = jnp.where(hit, NEG_INF, gate)
    return sel


def _attn_kernel(qi_ref, kj_ref, last_ref, bias_smem, lam_smem,
                 qt_ref, kb_ref, vt_ref, kmean_ref,
                 om_ref, od_ref,
                 bias_scr, qp_scr, sel_scr, m_scr, l_scr, accm_scr, accd_scr, s_scr):
    b = pl.program_id(0)
    p = pl.program_id(1)
    qi = qi_ref[p]
    kj = kj_ref[p]
    d = qi - kj
    tq = qt_ref.shape[1]
    tk = kb_ref.shape[0]

    @pl.when((b == 0) & (p == 0))
    def _build_bias():
        c = lax.broadcasted_iota(jnp.int32, (tk, tq), 0)
        r = lax.broadcasted_iota(jnp.int32, (tk, tq), 1)
        for dd in range(2):
            rel = dd * tq + r - c
            bucket = _rel_bucket(rel)
            for m in range(N_MAPS):
                bt = jnp.zeros((tk, tq), F32)
                for k in range(REL_BUCKETS):
                    bt = jnp.where(bucket == k, bias_smem[k, _bias_col(m)], bt)
                if dd == 0:
                    bt = jnp.where(rel >= 0, bt, NEG_INF)
                bias_scr[m, dd] = bt * LOG2E
        for m in range(N_MAPS):
            bias_scr[m, 2] = jnp.full((tk, tq), bias_smem[REL_BUCKETS - 1, _bias_col(m)], F32) * LOG2E

    @pl.when(d == 0)
    def _new_q_tile():
        half = lax.broadcasted_iota(jnp.int32, (2 * HEAD_DIM, tq), 0) // HEAD_DIM
        for j in range(N_MAPS // 2):
            qpair = qt_ref[j * 2 * HEAD_DIM:(j + 1) * 2 * HEAD_DIM, :].astype(F32)
            for e in range(2):
                qp_scr[2 * j + e] = jnp.where(half == e, qpair, 0.0).astype(qp_scr.dtype)
        blk = lax.broadcasted_iota(jnp.int32, (kmean_ref.shape[1], tq), 0)
        qblk = qi
        kmean = kmean_ref[0]
        for h in range(MOBA_HEADS):
            j = h // 2
            gate = _dot(kmean[:, j * 2 * HEAD_DIM:(j + 1) * 2 * HEAD_DIM], qp_scr[h])
            gate = jnp.where(blk < qblk, gate, NEG_INF)
            sel = _top_select(gate, blk, 0, MOBA_TOPK, qblk)
            sel_scr[h] = jnp.where(blk == qblk, 1.0, sel)
        m_scr[...] = jnp.full(m_scr.shape, NEG_INF, F32)
        l_scr[...] = jnp.zeros(l_scr.shape, F32)
        accm_scr[...] = jnp.zeros(accm_scr.shape, F32)
        accd_scr[...] = jnp.zeros(accd_scr.shape, F32)

    def scores(m):
        j = m // 2
        k_pair = kb_ref[:, j * 2 * HEAD_DIM:(j + 1) * 2 * HEAD_DIM]
        return jnp.dot(k_pair, qp_scr[m], preferred_element_type=F32)

    dsel = jnp.minimum(d, 2)
    ones = jnp.ones((2 * SUBLANES, tk), MM)
    n_buf = s_scr.shape[0]
    for m in range(n_buf - 1):
        s_scr[m] = scores(m)
    for m in range(N_MAPS):
        if m + n_buf - 1 < N_MAPS:
            s_scr[(m + n_buf - 1) % n_buf] = scores(m + n_buf - 1)
        s = s_scr[m % n_buf] + bias_scr[m, dsel]
        if m < MOBA_HEADS:
            s = jnp.where(sel_scr[m, pl.ds(kj, 1), :] > 0.5, s, NEG_INF)
            vt = vt_ref[m * HEAD_DIM:(m + 1) * HEAD_DIM, :]
        else:
            hv = (m - MOBA_HEADS) // 2
            vt = vt_ref[MOBA_W + hv * DIFF_VDIM:MOBA_W + (hv + 1) * DIFF_VDIM, :]
        m_prev = m_scr[m]
        m_new = jnp.maximum(m_prev, jnp.max(s, axis=0, keepdims=True))
        alpha = jnp.exp2(m_prev - m_new)
        pexp = jnp.exp2((s - m_new).astype(MM))
        m_scr[m] = m_new
        dv = vt.shape[0]
        pv = jnp.dot(jnp.concatenate([vt, ones], axis=0), pexp, preferred_element_type=F32)
        l_scr[m] = alpha * l_scr[m] + pv[dv:dv + 1]
        if m < MOBA_HEADS:
            accm_scr[m] = alpha * accm_scr[m] + pv[:dv]
        else:
            accd_scr[m - MOBA_HEADS] = alpha * accd_scr[m - MOBA_HEADS] + pv[:dv]

    @pl.when(last_ref[p] == 1)
    def _finish():
        lam = lam_smem[0]
        omt = jnp.concatenate([accm_scr[h] / l_scr[h] for h in range(MOBA_HEADS)], axis=0)
        om_ref[...] = omt.T
        outs = []
        for h in range(DIFF_HEADS):
            o0 = accd_scr[2 * h] / l_scr[MOBA_HEADS + 2 * h]
            o1 = accd_scr[2 * h + 1] / l_scr[MOBA_HEADS + 2 * h + 1]
            outs.append(o0 - lam * o1)
        od_ref[...] = jnp.concatenate(outs, axis=0).T


SCORE_BUFS = 5


def _attn_schedule(nq):
    qi, kj, last = [], [], []
    for i in range(nq):
        order = [i] + list(range(i))
        for n, j in enumerate(order):
            qi.append(i)
            kj.append(j)
            last.append(1 if n == len(order) - 1 else 0)
    return (jnp.asarray(qi, jnp.int32), jnp.asarray(kj, jnp.int32), jnp.asarray(last, jnp.int32))


def _prompt_attn(qt, kb, vt, kmean, rel_bias, lam, bsz):
    nbp = kmean.shape[1]
    s = kb.shape[0] // bsz
    tq = tk = MOBA_BLOCK
    nq = s // tq
    qi, kj, last = _attn_schedule(nq)
    qcol = pl.BlockSpec((QK_W, tq), lambda b, p, qi, kj, last: (0, b * nq + qi[p]))
    kcol = pl.BlockSpec((QK_W, tk), lambda b, p, qi, kj, last: (0, b * nq + kj[p]))
    krow = pl.BlockSpec((tk, QK_W), lambda b, p, qi, kj, last: (b * nq + kj[p], 0))
    orow = pl.BlockSpec((tq, MOBA_W), lambda b, p, qi, kj, last: (b * nq + qi[p], 0))
    smem = pl.BlockSpec(memory_space=pltpu.SMEM)
    grid_spec = pltpu.PrefetchScalarGridSpec(
        num_scalar_prefetch=3,
        grid=(bsz, int(qi.shape[0])),
        in_specs=[smem, smem, qcol, krow, kcol,
                  pl.BlockSpec((1, nbp, MOBA_W), lambda b, p, qi, kj, last: (b, 0, 0))],
        out_specs=[orow, orow],
        scratch_shapes=[
            pltpu.VMEM((N_MAPS, 3, tk, tq), F32),
            pltpu.VMEM((N_MAPS, 2 * HEAD_DIM, tq), MM),
            pltpu.VMEM((MOBA_HEADS, nbp, tq), F32),
            pltpu.VMEM((N_MAPS, 1, tq), F32),
            pltpu.VMEM((N_MAPS, 1, tq), F32),
            pltpu.VMEM((MOBA_HEADS, HEAD_DIM, tq), F32),
            pltpu.VMEM((2 * DIFF_HEADS, DIFF_VDIM, tq), F32),
            pltpu.VMEM((SCORE_BUFS, tk, tq), F32),
        ])
    t = kb.shape[0]
    return pl.pallas_call(
        _attn_kernel, grid_spec=grid_spec,
        out_shape=[jax.ShapeDtypeStruct((t, MOBA_W), F32), jax.ShapeDtypeStruct((t, DIFF_W), F32)],
        compiler_params=_cparams("arbitrary", "arbitrary"), name="attn",
    )(qi, kj, last, rel_bias, lam, qt, kb, vt, kmean)


PAGES_PER_STEP = 16
PAGE_ROWS = 128
TOKEN_ROWS = SUBLANES
MAPS_PER_TOKEN = 8
SROWS = MAPS_PER_TOKEN * TOKEN_ROWS


def _sattn_kernel(pt_ref, bias_smem, lam_smem,
                  qm_ref, qd_ref, kmn_ref, vmn_ref, kdn_ref, vdn_ref, *refs, tn):
    n = PAGES_PER_STEP
    mk_pages, mv_pages = refs[0:n], refs[n:2 * n]
    dk_pages, dv_pages = refs[2 * n:3 * n], refs[3 * n:4 * n]
    om_ref, od_ref = refs[4 * n], refs[4 * n + 1]
    qbm_scr, qbd_scr, gate_scr, sel_scr, m_scr, l_scr, far_scr, accm_scr, accd_scr = refs[4 * n + 2:]
    b = pl.program_id(0)
    ph = pl.program_id(1)
    j = pl.program_id(2)
    n_steps = pl.num_programs(2)
    tk = n * PAGE_ROWS
    past_len = n_steps * tk
    bps = tk // MOBA_BLOCK
    n_past = past_len // MOBA_BLOCK
    rows = SROWS

    row_m = lax.broadcasted_iota(jnp.int32, (rows, 1), 0) // TOKEN_ROWS
    row_t = lax.broadcasted_iota(jnp.int32, (rows, 1), 0) % TOKEN_ROWS
    moba_cols = [_bias_col(m) for m in range(MOBA_HEADS)]
    diff_cols = [_bias_col(MOBA_HEADS + m) for m in range(2 * DIFF_HEADS)]

    def bias_rows(rel, cols):
        bucket = _rel_bucket(rel)
        out = jnp.zeros(rel.shape, F32)
        for k in range(REL_BUCKETS):
            tab = jnp.zeros((rows, 1), F32)
            for mm, cc in enumerate(cols):
                tab = jnp.where(row_m == mm, bias_smem[k, cc], tab)
            out = jnp.where(bucket == k, tab, out)
        return out * LOG2E

    def page_bias(branch, rel, cols):
        far = jnp.broadcast_to(far_scr[branch], rel.shape)

        def last_step():
            near = bias_rows(rel[:, tk - PAGE_ROWS:], cols)
            return jnp.concatenate([far[:, :tk - PAGE_ROWS], near], axis=1)
        return lax.cond(j == n_steps - 1, last_step, lambda: far)

    def kt_step(pages):
        return jnp.concatenate([r[0, 0].reshape(MOBA_W, PAGE_ROWS) for r in pages], axis=-1).astype(MM)

    @pl.when((ph == 0) & (j == 0))
    def _start():
        def keep_own_segment(q):
            lane_seg = lax.broadcasted_iota(jnp.int32, q.shape, 1) // HEAD_DIM
            return jnp.where(lane_seg == row_m, q, 0.0)
        qbm_scr[...] = keep_own_segment(qm_ref[0]).astype(qbm_scr.dtype)
        qbd_scr[...] = keep_own_segment(qd_ref[0]).astype(qbd_scr.dtype)
        gate_scr[...] = jnp.full(gate_scr.shape, NEG_INF, F32)
        far = jnp.full((rows, 1), REL_MAX_DIST, jnp.int32)
        far_scr[0] = bias_rows(far, moba_cols)
        far_scr[1] = bias_rows(far, diff_cols)

    @pl.when(ph == 0)
    def _gate():
        s = jnp.dot(qbm_scr[...], kt_step(mk_pages), preferred_element_type=F32)
        col = lax.broadcasted_iota(jnp.int32, (rows, LANES), 1)
        g = gate_scr[...]
        for c in range(bps):
            bs = jnp.sum(s[:, c * MOBA_BLOCK:(c + 1) * MOBA_BLOCK], axis=-1, keepdims=True)
            g = jnp.where(col == j * bps + c, bs, g)
        gate_scr[...] = g

        @pl.when(j == n_steps - 1)
        def _select():
            sel = _top_select(g, col, 1, min(MOBA_TOPK, n_past), jnp.full((rows, 1), n_past, jnp.int32))
            sel_scr[...] = sel.astype(sel_scr.dtype)
            m_scr[...] = jnp.full(m_scr.shape, NEG_INF, F32)
            l_scr[...] = jnp.zeros(l_scr.shape, F32)
            accm_scr[...] = jnp.zeros(accm_scr.shape, F32)
            accd_scr[...] = jnp.zeros(accd_scr.shape, F32)

    def softmax_step(branch, s):
        m_prev = m_scr[branch]
        m_new = jnp.maximum(m_prev, jnp.max(s, axis=-1, keepdims=True))
        m_safe = jnp.where(m_new == NEG_INF, 0.0, m_new)
        alpha = jnp.exp2(m_prev - m_safe)
        pexp = jnp.exp2(s - m_safe)
        l_scr[branch] = alpha * l_scr[branch] + jnp.sum(pexp, axis=-1, keepdims=True)
        m_scr[branch] = m_new
        return alpha, pexp.astype(MM)

    def diff_pv(alpha, pexp, v_of_head):
        for h in range(DIFF_HEADS):
            rs = slice(h * 2 * TOKEN_ROWS, (h + 1) * 2 * TOKEN_ROWS)
            accd_scr[rs] = alpha[rs] * accd_scr[rs] + jnp.dot(pexp[rs], v_of_head(h), preferred_element_type=F32)

    @pl.when(ph == 1)
    def _attend():
        kpos = j * tk + lax.broadcasted_iota(jnp.int32, (rows, tk), 1)
        rel = past_len + row_t - kpos
        blk_of_col = (lax.broadcasted_iota(jnp.int32, (LANES, tk), 1) // MOBA_BLOCK) + j * bps
        onehot = (lax.broadcasted_iota(jnp.int32, (LANES, tk), 0) == blk_of_col).astype(sel_scr.dtype)
        allowed = jnp.dot(sel_scr[...], onehot, preferred_element_type=F32)
        s_m = jnp.dot(qbm_scr[...], kt_step(mk_pages), preferred_element_type=F32) + page_bias(0, rel, moba_cols)
        alpha, pexp = softmax_step(0, jnp.where(allowed > 0.5, s_m, NEG_INF))
        accm_scr[...] = alpha * accm_scr[...] + _dot_nt(pexp, kt_step(mv_pages))
        s_d = jnp.dot(qbd_scr[...], kt_step(dk_pages), preferred_element_type=F32) + page_bias(1, rel, diff_cols)
        alpha, pexp = softmax_step(1, s_d)
        diff_pv(alpha, pexp, lambda h: jnp.concatenate(
            [r[0, 0, pl.ds(h, PAGE_ROWS, stride=DIFF_HEADS), :] for r in dv_pages], axis=0).astype(MM))

        @pl.when(j == n_steps - 1)
        def _finish():
            ncol = kmn_ref.shape[1]
            col = lax.broadcasted_iota(jnp.int32, (rows, ncol), 1)
            reln = row_t - (col - b * tn)
            ok = (col >= b * tn) & (col < (b + 1) * tn) & (reln >= 0)
            s_mn = _dot(qbm_scr[...], kmn_ref[...]) + bias_rows(reln, moba_cols)
            alpha, pexp = softmax_step(0, jnp.where(ok, s_mn, NEG_INF))
            accm = alpha * accm_scr[...] + _dot_nt(pexp, vmn_ref[...])
            s_dn = _dot(qbd_scr[...], kdn_ref[...]) + bias_rows(reln, diff_cols)
            alpha, pexp = softmax_step(1, jnp.where(ok, s_dn, NEG_INF))
            diff_pv(alpha, pexp, lambda h: vdn_ref[:, h * DIFF_VDIM:(h + 1) * DIFF_VDIM].astype(MM))
            lam = lam_smem[0]
            om = accm / l_scr[0]
            for h in range(MOBA_HEADS):
                om_ref[0, :, h * HEAD_DIM:(h + 1) * HEAD_DIM] = (
                    om[h * TOKEN_ROWS:h * TOKEN_ROWS + tn, h * HEAD_DIM:(h + 1) * HEAD_DIM])
            od = accd_scr[...] / l_scr[1]
            for h in range(DIFF_HEADS):
                o0 = od[2 * h * TOKEN_ROWS:2 * h * TOKEN_ROWS + tn]
                o1 = od[(2 * h + 1) * TOKEN_ROWS:(2 * h + 1) * TOKEN_ROWS + tn]
                od_ref[0, :, h * DIFF_VDIM:(h + 1) * DIFF_VDIM] = o0 - lam * o1


def _sample_attn(qm, qd, kmn, vmn, kdn, vdn, caches, page_table, layer, rel_bias, lam, tn):
    nb = qm.shape[0]
    n_pages = page_table.shape[1]
    n_steps = n_pages // PAGES_PER_STEP
    ncol = kmn.shape[1]
    qspec = pl.BlockSpec((1, SROWS, MOBA_W), lambda b, ph, j, pt: (b, 0, 0))
    newt = pl.BlockSpec((MOBA_W, ncol), lambda b, ph, j, pt: (0, 0))
    newr = pl.BlockSpec((ncol, DIFF_W), lambda b, ph, j, pt: (0, 0))
    out = pl.BlockSpec((1, tn, MOBA_W), lambda b, ph, j, pt: (b, 0, 0))
    smem = pl.BlockSpec(memory_space=pltpu.SMEM)

    def page_specs(cache, both_phases):
        zeros = (0,) * (cache.ndim - 2)

        def index_map(b, ph, j, pt, g):
            step = j if both_phases else j * ph
            return (layer, pt[b, step * PAGES_PER_STEP + g]) + zeros
        return [pl.BlockSpec((1, 1) + cache.shape[2:], functools.partial(index_map, g=g))
                for g in range(PAGES_PER_STEP)]

    mk, mv, dk, dv = caches
    grid_spec = pltpu.PrefetchScalarGridSpec(
        num_scalar_prefetch=1, grid=(nb, 2, n_steps),
        in_specs=([smem, smem, qspec, qspec, newt, newt, newt, newr]
                  + page_specs(mk, True) + page_specs(mv, False) + page_specs(dk, False) + page_specs(dv, False)),
        out_specs=[out, out],
        scratch_shapes=[
            pltpu.VMEM((SROWS, MOBA_W), MM), pltpu.VMEM((SROWS, MOBA_W), MM),
            pltpu.VMEM((SROWS, LANES), F32), pltpu.VMEM((SROWS, LANES), MM),
            pltpu.VMEM((2, SROWS, 1), F32), pltpu.VMEM((2, SROWS, 1), F32), pltpu.VMEM((2, SROWS, 1), F32),
            pltpu.VMEM((SROWS, MOBA_W), F32), pltpu.VMEM((SROWS, DIFF_VDIM), F32),
        ])
    args = []
    for c in caches:
        args += [c] * PAGES_PER_STEP
    return pl.pallas_call(
        functools.partial(_sattn_kernel, tn=tn), grid_spec=grid_spec,
        out_shape=[jax.ShapeDtypeStruct((nb, tn, MOBA_W), F32), jax.ShapeDtypeStruct((nb, tn, DIFF_W), F32)],
        compiler_params=_cparams("arbitrary", "arbitrary", "arbitrary"), name="sattn",
    )(page_table, rel_bias, lam, qm, qd, kmn, vmn, kdn, vdn, *args)


def _merge_kernel(sub_scale, h_ref, om_ref, od_ref, gsub_ref, wout_ref, gffn_ref, wpq_ref, h1_ref, xn_ref, q_ref):
    od = od_ref[...]
    parts = [om_ref[...]]
    for h in range(DIFF_HEADS):
        parts.append(_rms(od[:, h * DIFF_VDIM:(h + 1) * DIFF_VDIM], gsub_ref[...]) * sub_scale)
    y = jnp.concatenate(parts, axis=-1)
    h1 = h_ref[...] + _dot(y, wout_ref[...])
    h1_ref[...] = h1
    xn = _rms(h1, gffn_ref[...])
    xn_ref[...] = xn.astype(xn_ref.dtype)
    q_ref[...] = _dot(xn, wpq_ref[...]).astype(q_ref.dtype)


def _merge(h, om, od, g_sub, w_out, g_ffn, w_pq, sub_scale, tt):
    t, d = h.shape
    nq = w_pq.shape[1]
    row = lambda i: (i, 0)
    const = lambda i: (0, 0)
    return pl.pallas_call(
        functools.partial(_merge_kernel, sub_scale),
        grid=(t // tt,),
        in_specs=[pl.BlockSpec((tt, d), row), pl.BlockSpec((tt, MOBA_W), row), pl.BlockSpec((tt, DIFF_W), row),
                  pl.BlockSpec((1, DIFF_VDIM), const), pl.BlockSpec(w_out.shape, const),
                  pl.BlockSpec((1, d), const), pl.BlockSpec(w_pq.shape, const)],
        out_specs=[pl.BlockSpec((tt, d), row), pl.BlockSpec((tt, d), row), pl.BlockSpec((tt, nq), row)],
        out_shape=[jax.ShapeDtypeStruct((t, d), F32), jax.ShapeDtypeStruct((t, d), MM),
                   jax.ShapeDtypeStruct((t, nq), MM)],
        compiler_params=_cparams("parallel"), name="merge",
    )(h, om, od, g_sub, w_out, g_ffn, w_pq)


def _topk_rows(s, k, want_rank):
    n = s.shape[0]
    row = lax.broadcasted_iota(jnp.int32, s.shape, 0)
    rank = jnp.full(s.shape, k, jnp.int32) if want_rank else None
    vals = jnp.zeros((k, s.shape[1]), F32)
    idxs = jnp.zeros((k, s.shape[1]), jnp.int32)
    vrow = lax.broadcasted_iota(jnp.int32, vals.shape, 0)
    for r in range(k):
        m = jnp.max(s, axis=0, keepdims=True)
        idx = jnp.min(jnp.where(s == m, row, n), axis=0, keepdims=True)
        hit = row == idx
        if want_rank:
            rank = jnp.where(hit, r, rank)
        s = jnp.where(hit, NEG_INF, s)
        vals = jnp.where(vrow == r, m, vals)
        idxs = jnp.where(vrow == r, idx, idxs)
    return vals, idxs, rank


def _top_pair_sums(v1, v2, k):
    row = lax.broadcasted_iota(jnp.int32, v1.shape, 0)
    p = jnp.zeros(v1.shape, jnp.int32)
    g = jnp.broadcast_to(v2[0:1], v1.shape)
    top = v1[0:1] + v2[0:1]
    z = jnp.zeros(top.shape, F32)
    for _ in range(k):
        f = v1 + g
        m = jnp.max(f, axis=0, keepdims=True)
        a = jnp.min(jnp.where(f == m, row, k), axis=0, keepdims=True)
        hit = row == a
        z = z + jnp.exp(m - top)
        p = jnp.where(hit, p + 1, p)
        nxt = jnp.max(jnp.where(hit, p, -1), axis=0, keepdims=True)
        v_nxt = jnp.max(jnp.where(row == nxt, v2, NEG_INF), axis=0, keepdims=True)
        g = jnp.where(hit, v_nxt, g)
    return p, z


def _peersel_kernel(q_ref, keys_ref, rank2_ref, cnt_ref, e1_ref, e2_ref):
    k = PEER_TOPK
    q = q_ref[...]
    s1 = _dot_nt(keys_ref[0, 0], q[:, :PEER_HALF])
    s2 = _dot_nt(keys_ref[0, 1], q[:, PEER_HALF:])
    v1, idx1, _ = _topk_rows(s1, k, False)
    v2, _, rank2 = _topk_rows(s2, k, True)
    picks, z = _top_pair_sums(v1, v2, k)
    key = lax.broadcasted_iota(jnp.int32, s1.shape, 0)
    cnt = jnp.zeros(s1.shape, jnp.int32)
    for r in range(k):
        cnt = jnp.where(key == idx1[r:r + 1], picks[r:r + 1], cnt)
    rank2_ref[0] = rank2.astype(rank2_ref.dtype)
    cnt_ref[0] = cnt.astype(cnt_ref.dtype)
    e1_ref[0] = (jnp.exp(s1 - v1[0:1]) / z).astype(e1_ref.dtype)
    e2_ref[0] = jnp.exp(s2 - v2[0:1]).astype(e2_ref.dtype)


def _peersel(q, sub_keys, tt):
    t = q.shape[0]
    spec = pl.BlockSpec((1, PEER_NKEYS, tt), lambda i, h: (h, 0, i))
    tile = jax.ShapeDtypeStruct((PEER_HEADS, PEER_NKEYS, t), MM)
    rows = jax.ShapeDtypeStruct((PEER_HEADS, PEER_NKEYS, t), F32)
    return pl.pallas_call(
        _peersel_kernel,
        grid=(t // tt, PEER_HEADS),
        in_specs=[pl.BlockSpec((tt, 2 * PEER_HALF), lambda i, h: (i, h)),
                  pl.BlockSpec((1, 2, PEER_NKEYS, PEER_HALF), lambda i, h: (h, 0, 0, 0))],
        out_specs=[spec] * 4, out_shape=[tile, rows, rows, tile],
        compiler_params=_cparams("parallel", "arbitrary"), name="peersel",
    )(q, sub_keys)


PEER_I1_PER_STEP = SUBLANES
PEER_TOKEN_SPLIT = 2
PEER_TOKEN_TILE = 512


def _peerffn_kernel(xn_ref, u_ref, vt_ref, rank2_ref, cnt_ref, e1_ref, e2_ref, out_ref, hid_scr, coef_scr, acc_scr):
    n_split = hid_scr.shape[0]
    j = pl.program_id(1)

    @pl.when(j == 0)
    def _zero():
        acc_scr[...] = jnp.zeros(acc_scr.shape, F32)

    tt = xn_ref.shape[0]
    th = tt // n_split
    wdt = rank2_ref.dtype
    pack = 4 // jnp.dtype(wdt).itemsize * SUBLANES
    grp = (PEER_NKEYS // pack, pack, th)

    def hidden(hf):
        hid_scr[hf] = _dot_nt(u_ref[...], xn_ref[hf * th:(hf + 1) * th, :])

    def weights(hf):
        ts = slice(hf * th, (hf + 1) * th)
        for a in range(PEER_I1_PER_STEP):
            rs = slice(a * PEER_NKEYS, (a + 1) * PEER_NKEYS)
            hid = hid_scr[hf, rs]
            act = 0.5 * hid * (1.0 + lax.erf(hid * (2.0 ** -0.5)))
            w = jnp.zeros(grp, wdt)
            for h in range(PEER_HEADS):
                cnt = jnp.broadcast_to(cnt_ref[h, a:a + 1, ts], (pack, th)).astype(wdt)
                e1 = jnp.broadcast_to(e1_ref[h, a:a + 1, ts], (pack, th)).astype(wdt)
                w = w + jnp.where(rank2_ref[h, :, ts].reshape(grp) < cnt, e2_ref[h, :, ts].reshape(grp),
                                  jnp.zeros_like(w)) * e1
            coef_scr[hf, rs] = w.reshape(PEER_NKEYS, th) * act.astype(wdt)

    def project(hf):
        ts = slice(hf * th, (hf + 1) * th)
        acc_scr[:, ts] += jnp.dot(vt_ref[...], coef_scr[hf], preferred_element_type=F32)

    hidden(0)
    for hf in range(n_split):
        if hf + 1 < n_split:
            hidden(hf + 1)
        weights(hf)
        project(hf)

    @pl.when(j == pl.num_programs(1) - 1)
    def _store():
        out_ref[...] = acc_scr[...].T


def _peerffn(xn, u, vt, rank2, cnt, e1, e2, tt):
    t, d = xn.shape
    ne = PEER_I1_PER_STEP * PEER_NKEYS
    n_split = PEER_TOKEN_SPLIT if tt % (PEER_TOKEN_SPLIT * LANES) == 0 else 1
    th = tt // n_split
    tile = pl.BlockSpec((PEER_HEADS, PEER_NKEYS, tt), lambda i, j: (0, 0, i))
    rows = pl.BlockSpec((PEER_HEADS, PEER_I1_PER_STEP, tt), lambda i, j: (0, j, i))
    return pl.pallas_call(
        _peerffn_kernel,
        grid=(t // tt, PEER_NKEYS // PEER_I1_PER_STEP),
        in_specs=[pl.BlockSpec((tt, d), lambda i, j: (i, 0)),
                  pl.BlockSpec((ne, d), lambda i, j: (j, 0)), pl.BlockSpec((d, ne), lambda i, j: (0, j)),
                  tile, rows, rows, tile],
        out_specs=pl.BlockSpec((tt, d), lambda i, j: (i, 0)),
        out_shape=jax.ShapeDtypeStruct((t, d), F32),
        scratch_shapes=[pltpu.VMEM((n_split, ne, th), F32), pltpu.VMEM((n_split, ne, th), rank2.dtype),
                        pltpu.VMEM((d, tt), F32)],
        compiler_params=_cparams("parallel", "arbitrary"), name="peerffn",
    )(xn, u, vt, rank2, cnt, e1, e2)


def _ple_kernel(h_ref, peer_ref, p_ref, g_ref, wg_ref, wp_ref, out_ref):
    h2 = h_ref[...] + peer_ref[...]
    gate = jax.nn.sigmoid(_dot(_rms(h2, g_ref[...]), wg_ref[...]))
    out_ref[...] = h2 + gate * _dot(p_ref[...], wp_ref[...])


def _ple(h1, peer, p, g_ple, w_pg, w_pp, tt):
    t, d = h1.shape
    row = lambda i: (i, 0)
    const = lambda i: (0, 0)
    return pl.pallas_call(
        _ple_kernel,
        grid=(t // tt,),
        in_specs=[pl.BlockSpec((tt, d), row), pl.BlockSpec((tt, d), row), pl.BlockSpec((tt, p.shape[1]), row),
                  pl.BlockSpec((1, d), const), pl.BlockSpec(w_pg.shape, const), pl.BlockSpec(w_pp.shape, const)],
        out_specs=pl.BlockSpec((tt, d), row),
        out_shape=jax.ShapeDtypeStruct((t, d), F32),
        compiler_params=_cparams("parallel"), name="ple",
    )(h1, peer, p, g_ple, w_pg, w_pp)


def _token_tile(t, pref):
    return pref if t % pref == 0 else t


def kernel(x_prompt, x_sample, cache_moba_k, cache_moba_v, cache_diff_k, cache_diff_v, page_table, p_prompt, p_sample, rel_bias, g_attn, w_in, g_q_moba, g_k_moba, g_q_diff, g_k_diff, lam_q1, lam_k1, lam_q2, lam_k2, g_subln, w_out, g_ffn, w_peer_q, peer_sub_keys, peer_u, peer_v, g_ple, w_ple_gate, w_ple_proj):
    bsz, seq, d = x_prompt.shape
    nb, tn, _ = x_sample.shape
    depth = w_in.shape[0]
    page = cache_moba_k.shape[2]
    assert page == PAGE_ROWS and seq % MOBA_BLOCK == 0 and tn <= TOKEN_ROWS and REL_MAX_DIST <= PAGE_ROWS
    past_len = page_table.shape[1] * page
    assert past_len % (PAGES_PER_STEP * page) == 0 and past_len // MOBA_BLOCK <= LANES

    seg = jnp.asarray(np.kron(np.eye(MOBA_W // HEAD_DIM), np.full((HEAD_DIM, HEAD_DIM), 1.0 / HEAD_DIM)), jnp.bfloat16)
    caches = (jnp.transpose(cache_moba_k, (0, 1, 3, 4, 2)), jnp.transpose(cache_moba_v, (0, 1, 3, 4, 2)),
              jnp.transpose(cache_diff_k, (0, 1, 3, 4, 5, 2)),
              cache_diff_v.reshape(depth, cache_diff_v.shape[1], page * DIFF_HEADS, DIFF_VDIM))
    tile8 = lambda g: jnp.tile(g, MOBA_W // HEAD_DIM)

    hp = x_prompt.reshape(bsz * seq, d)
    hs = x_sample.reshape(nb * tn, d)
    new_p = [[] for _ in range(4)]
    new_s = [[] for _ in range(4)]
    w0, w1, w2, w3, w4, w5 = (slice(c * MOBA_W, (c + 1) * MOBA_W) for c in range(6))
    for i in range(depth):
        lam_init = 0.8 - 0.6 * math.exp(-0.3 * i)
        lam = (jnp.exp(jnp.sum(lam_q1[i] * lam_k1[i])) - jnp.exp(jnp.sum(lam_q2[i] * lam_k2[i])) + lam_init)
        lam = lam.reshape(1).astype(F32)
        wi = w_in[i]
        wkv = jnp.concatenate([wi[:, w1], wi[:, w4], wi[:, w2], wi[:, w5]], axis=1).astype(MM)
        wqt = jnp.concatenate([wi[:, w0], wi[:, w3]], axis=1).T.astype(MM)
        wvt = jnp.concatenate([wi[:, w2], wi[:, w5]], axis=1).T.astype(MM)
        gk = jnp.concatenate([tile8(g_k_moba[i]), tile8(g_k_diff[i])])[None, :]
        gqt = (jnp.concatenate([tile8(g_q_moba[i]), tile8(g_q_diff[i])]) * (ATTN_SCALE * LOG2E))[:, None]
        w_out_i = w_out[i].astype(MM)
        w_pq_i = w_peer_q[i].astype(MM)
        u_i = peer_u[i].astype(MM)
        vt_i = peer_v[i].T.astype(MM)
        w_pg_i = w_ple_gate[i].astype(MM)
        w_pp_i = w_ple_proj[i].astype(MM)

        tp = _token_tile(bsz * seq, 512)
        kmt, vmt, kdt, vd, kb, qt, vt, kmean = _proj(hp, g_attn[i][None], wkv, wqt, wvt, seg, gk, gqt, tp, bsz, True)
        for lst, a in zip(new_p, (kmt, vmt, kdt, vd)):
            lst.append(a)
        kmean = kmean.reshape(bsz, seq // MOBA_BLOCK, MOBA_W)
        kmean = jnp.pad(kmean, ((0, 0), (0, -kmean.shape[1] % SUBLANES), (0, 0)))
        om_p, od_p = _prompt_attn(qt, kb, vt, kmean, rel_bias, lam, bsz)

        ts = _token_tile(nb * tn, 128)
        kmt, vmt, kdt, vd, _, qt, _ = _proj(hs, g_attn[i][None], wkv, wqt, wvt, seg, gk, gqt, ts, 1, False)
        for lst, a in zip(new_s, (kmt, vmt, kdt, vd)):
            lst.append(a)
        qs = qt.T.astype(F32).reshape(nb, 1, tn, QK_W)
        qs = jnp.pad(qs, ((0, 0), (0, 0), (0, TOKEN_ROWS - tn), (0, 0)))
        qs = jnp.broadcast_to(qs, (nb, MAPS_PER_TOKEN, TOKEN_ROWS, QK_W)).reshape(nb, SROWS, QK_W)
        om_s, od_s = _sample_attn(qs[:, :, :MOBA_W], qs[:, :, MOBA_W:], kmt[0], vmt[0], kdt[0], vd, caches,
                                  page_table, i, rel_bias, lam, tn)

        outs = []
        for h, om, od, p, tt in ((hp, om_p, od_p, p_prompt[i].reshape(bsz * seq, -1), tp),
                                 (hs, om_s.reshape(-1, MOBA_W), od_s.reshape(-1, DIFF_W),
                                  p_sample[i].reshape(nb * tn, -1), ts)):
            h1, xn, q = _merge(h, om, od, g_subln[i][None], w_out_i, g_ffn[i][None], w_pq_i, 1.0 - lam_init, tt)
            rank2, cnt, e1, e2 = _peersel(q, peer_sub_keys[i], tt)
            peer = _peerffn(xn, u_i, vt_i, rank2, cnt, e1, e2, _token_tile(xn.shape[0], PEER_TOKEN_TILE))
            outs.append(_ple(h1, peer, p, g_ple[i][None], w_pg_i, w_pp_i, tt))
        hp, hs = outs

    def stack(lst, lead, tail):
        def rows(a):
            if a.ndim == 3:
                a = jnp.moveaxis(a.reshape(a.shape[0], *tail, a.shape[2]), -1, 1)
            return a.reshape(*lead, *tail)
        return jnp.stack([rows(a) for a in lst])

    lp, ls = (bsz, seq), (nb, tn)
    tails = ((MOBA_HEADS, HEAD_DIM), (MOBA_HEADS, HEAD_DIM), (DIFF_HEADS, 2, HEAD_DIM), (DIFF_HEADS, DIFF_VDIM))
    return (hp.reshape(bsz, seq, d), hs.reshape(nb, tn, d),
            *[stack(l, lp, t) for l, t in zip(new_p, tails)],
            *[stack(l, ls, t) for l, t in zip(new_s, tails)])
```

```python
import functools
import math

import numpy as np
import jax
import jax.numpy as jnp
from jax import lax
from jax.experimental import pallas as pl
from jax.experimental.pallas import tpu as pltpu

F32 = jnp.float32
MM = jnp.bfloat16

HEAD_DIM = 64
MOBA_HEADS = 8
MOBA_BLOCK = 256
MOBA_TOPK = 3
DIFF_HEADS = 4
DIFF_VDIM = 2 * HEAD_DIM
MOBA_W = MOBA_HEADS * HEAD_DIM
DIFF_W = DIFF_HEADS * DIFF_VDIM
QK_W = MOBA_W + DIFF_W
N_MAPS = MOBA_HEADS + 2 * DIFF_HEADS
ATTN_SCALE = HEAD_DIM ** -0.5
LOG2E = math.log2(math.e)
REL_BUCKETS = 32
REL_MAX_EXACT = 16
REL_MAX_DIST = 128
PEER_HEADS = 8
PEER_NKEYS = 128
PEER_HALF = 128
PEER_TOPK = 16
RMS_EPS = 1e-6
LANES = 128
SUBLANES = 8
NEG_INF = float("-inf")
VMEM_LIMIT = 56 * 1024 * 1024


def _cparams(*sem):
    return pltpu.CompilerParams(dimension_semantics=sem, vmem_limit_bytes=VMEM_LIMIT)


def _rms(x, g):
    return x * lax.rsqrt(jnp.mean(x * x, axis=-1, keepdims=True) + RMS_EPS) * g


def _dot(a, b):
    return jnp.dot(a.astype(MM), b.astype(MM), preferred_element_type=F32)


def _dot_nt(a, b):
    return lax.dot_general(a.astype(MM), b.astype(MM), (((1,), (1,)), ((), ())), preferred_element_type=F32)


def _seg_mean(sq, seg):
    hi = sq.astype(jnp.bfloat16)
    lo = (sq - hi.astype(F32)).astype(jnp.bfloat16)
    return (jnp.dot(hi, seg, preferred_element_type=F32) + jnp.dot(lo, seg, preferred_element_type=F32))


def _rel_bucket(rel):
    n = jnp.maximum(rel, 0)
    nf = jnp.maximum(n, 1).astype(F32)
    large = REL_MAX_EXACT + (jnp.log(nf / REL_MAX_EXACT) / math.log(REL_MAX_DIST / REL_MAX_EXACT)
                             * (REL_BUCKETS - REL_MAX_EXACT)).astype(jnp.int32)
    large = jnp.minimum(large, REL_BUCKETS - 1)
    return jnp.where(n < REL_MAX_EXACT, n, large)


def _bias_col(m):
    return m if m < MOBA_HEADS else MOBA_HEADS + (m - MOBA_HEADS) // 2


def _proj_kernel(x_ref, g_ref, wkv_ref, wqt_ref, wvt_ref, seg_ref, gk_ref, gqt_ref,
                 kmt_ref, vmt_ref, kdt_ref, vd_ref, kb_ref, qt_ref, vt_ref, *maybe_kmean):
    xn = _rms(x_ref[...], g_ref[...]).astype(MM)
    z = jnp.dot(xn, wkv_ref[...], preferred_element_type=F32)
    seg = seg_ref[...]
    gk = gk_ref[...]
    ks = []
    for c in range(2):
        zc = z[:, c * MOBA_W:(c + 1) * MOBA_W]
        ks.append(zc * lax.rsqrt(_seg_mean(zc * zc, seg) + RMS_EPS) * gk[:, c * MOBA_W:(c + 1) * MOBA_W])
    kmt_ref[0] = ks[0].T
    kdt_ref[0] = ks[1].T
    kb_ref[...] = jnp.concatenate(ks, axis=-1).astype(kb_ref.dtype)
    vmt_ref[0] = z[:, 2 * MOBA_W:3 * MOBA_W].T
    vd_ref[...] = z[:, 3 * MOBA_W:4 * MOBA_W]
    tt = xn.shape[0]
    zq = _dot_nt(wqt_ref[...], xn).reshape(QK_W // HEAD_DIM, HEAD_DIM, tt)
    ms = jnp.mean(zq * zq, axis=1, keepdims=True)
    qt = (zq * lax.rsqrt(ms + RMS_EPS)).reshape(QK_W, tt) * gqt_ref[...]
    qt_ref[...] = qt.astype(qt_ref.dtype)
    vt_ref[...] = _dot_nt(wvt_ref[...], xn).astype(vt_ref.dtype)
    if maybe_kmean:
        kmean_ref, = maybe_kmean
        for c in range(tt // MOBA_BLOCK):
            kmean_ref[0, c:c + 1, :] = jnp.mean(ks[0][c * MOBA_BLOCK:(c + 1) * MOBA_BLOCK], axis=0, keepdims=True)


def _proj(x2d, g_attn, wkv, wqt, wvt, seg, gk, gqt, tt, n_seq, with_kmean):
    t, d = x2d.shape
    nt = t // tt
    tps = nt // n_seq
    row = lambda i: (i, 0)
    col = lambda i: (0, i)
    const = lambda i: (0, 0)
    full = lambda a: pl.BlockSpec(a.shape, const)
    feat = jax.ShapeDtypeStruct((n_seq, MOBA_W, t // n_seq), F32)
    feat_spec = pl.BlockSpec((1, MOBA_W, tt), lambda i: (i // tps, 0, i % tps))
    out_shape = [feat, feat, feat, jax.ShapeDtypeStruct((t, MOBA_W), F32),
                 jax.ShapeDtypeStruct((t, QK_W), MM), jax.ShapeDtypeStruct((QK_W, t), MM),
                 jax.ShapeDtypeStruct((QK_W, t), MM)]
    out_specs = [feat_spec, feat_spec, feat_spec, pl.BlockSpec((tt, MOBA_W), row),
                 pl.BlockSpec((tt, QK_W), row), pl.BlockSpec((QK_W, tt), col), pl.BlockSpec((QK_W, tt), col)]
    if with_kmean:
        nb = tt // MOBA_BLOCK
        out_shape.append(jax.ShapeDtypeStruct((nt, nb, MOBA_W), F32))
        out_specs.append(pl.BlockSpec((1, nb, MOBA_W), lambda i: (i, 0, 0)))
    return pl.pallas_call(
        _proj_kernel,
        grid=(nt,),
        in_specs=[pl.BlockSpec((tt, d), row), full(g_attn), full(wkv), full(wqt), full(wvt), full(seg), full(gk),
                  full(gqt)],
        out_specs=out_specs, out_shape=out_shape,
        compiler_params=_cparams("parallel"), name="proj",
    )(x2d, g_attn, wkv, wqt, wvt, seg, gk, gqt)


def _top_select(gate, pos, axis, n_sel, n_ok):
    sel = jnp.zeros(gate.shape, F32)
    big = gate.shape[axis]
    for k in range(n_sel):
        m = jnp.max(gate, axis=axis, keepdims=True)
        idx = jnp.min(jnp.where(gate == m, pos, big), axis=axis, keepdims=True)
        hit = pos == idx
        sel = jnp.where(hit & (n_ok > k), 1.0, sel)
        gate = jnp.where(hit, NEG_INF, gate)
    return sel


def _attn_kernel(qi_ref, kj_ref, last_ref, bias_smem, lam_smem,
                 qt_ref, kb_ref, vt_ref, kmean_ref,
                 om_ref, od_ref,
                 bias_scr, qp_scr, sel_scr, m_scr, l_scr, accm_scr, accd_scr, s_scr):
    b = pl.program_id(0)
    p = pl.program_id(1)
    qi = qi_ref[p]
    kj = kj_ref[p]
    d = qi - kj
    tq = qt_ref.shape[1]
    tk = kb_ref.shape[0]

    @pl.when((b == 0) & (p == 0))
    def _build_bias():
        c = lax.broadcasted_iota(jnp.int32, (tk, tq), 0)
        r = lax.broadcasted_iota(jnp.int32, (tk, tq), 1)
        for dd in range(2):
            rel = dd * tq + r - c
            bucket = _rel_bucket(rel)
            for m in range(N_MAPS):
                bt = jnp.zeros((tk, tq), F32)
                for k in range(REL_BUCKETS):
                    bt = jnp.where(bucket == k, bias_smem[k, _bias_col(m)], bt)
                if dd == 0:
                    bt = jnp.where(rel >= 0, bt, NEG_INF)
                bias_scr[m, dd] = bt * LOG2E
        for m in range(N_MAPS):
            bias_scr[m, 2] = jnp.full((tk, tq), bias_smem[REL_BUCKETS - 1, _bias_col(m)], F32) * LOG2E

    @pl.when(d == 0)
    def _new_q_tile():
        half = lax.broadcasted_iota(jnp.int32, (2 * HEAD_DIM, tq), 0) // HEAD_DIM
        for j in range(N_MAPS // 2):
            qpair = qt_ref[j * 2 * HEAD_DIM:(j + 1) * 2 * HEAD_DIM, :].astype(F32)
            for e in range(2):
                qp_scr[2 * j + e] = jnp.where(half == e, qpair, 0.0).astype(qp_scr.dtype)
        blk = lax.broadcasted_iota(jnp.int32, (kmean_ref.shape[1], tq), 0)
        qblk = qi
        kmean = kmean_ref[0]
        for h in range(MOBA_HEADS):
            j = h // 2
            gate = _dot(kmean[:, j * 2 * HEAD_DIM:(j + 1) * 2 * HEAD_DIM], qp_scr[h])
            gate = jnp.where(blk < qblk, gate, NEG_INF)
            sel = _top_select(gate, blk, 0, MOBA_TOPK, qblk)
            sel_scr[h] = jnp.where(blk == qblk, 1.0, sel)
        m_scr[...] = jnp.full(m_scr.shape, NEG_INF, F32)
        l_scr[...] = jnp.zeros(l_scr.shape, F32)
        accm_scr[...] = jnp.zeros(accm_scr.shape, F32)
        accd_scr[...] = jnp.zeros(accd_scr.shape, F32)

    def scores(m):
        j = m // 2
        k_pair = kb_ref[:, j * 2 * HEAD_DIM:(j + 1) * 2 * HEAD_DIM]
        return jnp.dot(k_pair, qp_scr[m], preferred_element_type=F32)

    dsel = jnp.minimum(d, 2)
    ones = jnp.ones((2 * SUBLANES, tk), MM)
    n_buf = s_scr.shape[0]
    for m in range(n_buf - 1):
        s_scr[m] = scores(m)
    for m in range(N_MAPS):
        if m + n_buf - 1 < N_MAPS:
            s_scr[(m + n_buf - 1) % n_buf] = scores(m + n_buf - 1)
        s = s_scr[m % n_buf] + bias_scr[m, dsel]
        if m < MOBA_HEADS:
            s = jnp.where(sel_scr[m, pl.ds(kj, 1), :] > 0.5, s, NEG_INF)
            vt = vt_ref[m * HEAD_DIM:(m + 1) * HEAD_DIM, :]
        else:
            hv = (m - MOBA_HEADS) // 2
            vt = vt_ref[MOBA_W + hv * DIFF_VDIM:MOBA_W + (hv + 1) * DIFF_VDIM, :]
        m_prev = m_scr[m]
        m_new = jnp.maximum(m_prev, jnp.max(s, axis=0, keepdims=True))
        alpha = jnp.exp2(m_prev - m_new)
        pexp = jnp.exp2((s - m_new).astype(MM))
        m_scr[m] = m_new
        dv = vt.shape[0]
        pv = jnp.dot(jnp.concatenate([vt, ones], axis=0), pexp, preferred_element_type=F32)
        l_scr[m] = alpha * l_scr[m] + pv[dv:dv + 1]
        if m < MOBA_HEADS:
            accm_scr[m] = alpha * accm_scr[m] + pv[:dv]
        else:
            accd_scr[m - MOBA_HEADS] = alpha * accd_scr[m - MOBA_HEADS] + pv[:dv]

    @pl.when(last_ref[p] == 1)
    def _finish():
        lam = lam_smem[0]
        omt = jnp.concatenate([accm_scr[h] / l_scr[h] for h in range(MOBA_HEADS)], axis=0)
        om_ref[...] = omt.T
        outs = []
        for h in range(DIFF_HEADS):
            o0 = accd_scr[2 * h] / l_scr[MOBA_HEADS + 2 * h]
            o1 = accd_scr[2 * h + 1] / l_scr[MOBA_HEADS + 2 * h + 1]
            outs.append(o0 - lam * o1)
        od_ref[...] = jnp.concatenate(outs, axis=0).T


SCORE_BUFS = 5


def _attn_schedule(nq):
    qi, kj, last = [], [], []
    for i in range(nq):
        order = [i] + list(range(i))
        for n, j in enumerate(order):
            qi.append(i)
            kj.append(j)
            last.append(1 if n == len(order) - 1 else 0)
    return (jnp.asarray(qi, jnp.int32), jnp.asarray(kj, jnp.int32), jnp.asarray(last, jnp.int32))


def _prompt_attn(qt, kb, vt, kmean, rel_bias, lam, bsz):
    nbp = kmean.shape[1]
    s = kb.shape[0] // bsz
    tq = tk = MOBA_BLOCK
    nq = s // tq
    qi, kj, last = _attn_schedule(nq)
    qcol = pl.BlockSpec((QK_W, tq), lambda b, p, qi, kj, last: (0, b * nq + qi[p]))
    kcol = pl.BlockSpec((QK_W, tk), lambda b, p, qi, kj, last: (0, b * nq + kj[p]))
    krow = pl.BlockSpec((tk, QK_W), lambda b, p, qi, kj, last: (b * nq + kj[p], 0))
    orow = pl.BlockSpec((tq, MOBA_W), lambda b, p, qi, kj, last: (b * nq + qi[p], 0))
    smem = pl.BlockSpec(memory_space=pltpu.SMEM)
    grid_spec = pltpu.PrefetchScalarGridSpec(
        num_scalar_prefetch=3,
        grid=(bsz, int(qi.shape[0])),
        in_specs=[smem, smem, qcol, krow, kcol,
                  pl.BlockSpec((1, nbp, MOBA_W), lambda b, p, qi, kj, last: (b, 0, 0))],
        out_specs=[orow, orow],
        scratch_shapes=[
            pltpu.VMEM((N_MAPS, 3, tk, tq), F32),
            pltpu.VMEM((N_MAPS, 2 * HEAD_DIM, tq), MM),
            pltpu.VMEM((MOBA_HEADS, nbp, tq), F32),
            pltpu.VMEM((N_MAPS, 1, tq), F32),
            pltpu.VMEM((N_MAPS, 1, tq), F32),
            pltpu.VMEM((MOBA_HEADS, HEAD_DIM, tq), F32),
            pltpu.VMEM((2 * DIFF_HEADS, DIFF_VDIM, tq), F32),
            pltpu.VMEM((SCORE_BUFS, tk, tq), F32),
        ])
    t = kb.shape[0]
    return pl.pallas_call(
        _attn_kernel, grid_spec=grid_spec,
        out_shape=[jax.ShapeDtypeStruct((t, MOBA_W), F32), jax.ShapeDtypeStruct((t, DIFF_W), F32)],
        compiler_params=_cparams("arbitrary", "arbitrary"), name="attn",
    )(qi, kj, last, rel_bias, lam, qt, kb, vt, kmean)


PAGES_PER_STEP = 16
PAGE_ROWS = 128
TOKEN_ROWS = SUBLANES
MAPS_PER_TOKEN = 8
SROWS = MAPS_PER_TOKEN * TOKEN_ROWS


def _sattn_kernel(pt_ref, bias_smem, lam_smem,
                  qm_ref, qd_ref, kmn_ref, vmn_ref, kdn_ref, vdn_ref, *refs, tn):
    n = PAGES_PER_STEP
    mk_pages, mv_pages = refs[0:n], refs[n:2 * n]
    dk_pages, dv_pages = refs[2 * n:3 * n], refs[3 * n:4 * n]
    om_ref, od_ref = refs[4 * n], refs[4 * n + 1]
    qbm_scr, qbd_scr, gate_scr, sel_scr, m_scr, l_scr, far_scr, accm_scr, accd_scr = refs[4 * n + 2:]
    b = pl.program_id(0)
    ph = pl.program_id(1)
    j = pl.program_id(2)
    n_steps = pl.num_programs(2)
    tk = n * PAGE_ROWS
    past_len = n_steps * tk
    bps = tk // MOBA_BLOCK
    n_past = past_len // MOBA_BLOCK
    rows = SROWS

    row_m = lax.broadcasted_iota(jnp.int32, (rows, 1), 0) // TOKEN_ROWS
    row_t = lax.broadcasted_iota(jnp.int32, (rows, 1), 0) % TOKEN_ROWS
    moba_cols = [_bias_col(m) for m in range(MOBA_HEADS)]
    diff_cols = [_bias_col(MOBA_HEADS + m) for m in range(2 * DIFF_HEADS)]

    def bias_rows(rel, cols):
        bucket = _rel_bucket(rel)
        out = jnp.zeros(rel.shape, F32)
        for k in range(REL_BUCKETS):
            tab = jnp.zeros((rows, 1), F32)
            for mm, cc in enumerate(cols):
                tab = jnp.where(row_m == mm, bias_smem[k, cc], tab)
            out = jnp.where(bucket == k, tab, out)
        return out * LOG2E

    def page_bias(branch, rel, cols):
        far = jnp.broadcast_to(far_scr[branch], rel.shape)

        def last_step():
            near = bias_rows(rel[:, tk - PAGE_ROWS:], cols)
            return jnp.concatenate([far[:, :tk - PAGE_ROWS], near], axis=1)
        return lax.cond(j == n_steps - 1, last_step, lambda: far)

    def kt_step(pages):
        return jnp.concatenate([r[0, 0].reshape(MOBA_W, PAGE_ROWS) for r in pages], axis=-1).astype(MM)

    @pl.when((ph == 0) & (j == 0))
    def _start():
        def keep_own_segment(q):
            lane_seg = lax.broadcasted_iota(jnp.int32, q.shape, 1) // HEAD_DIM
            return jnp.where(lane_seg == row_m, q, 0.0)
        qbm_scr[...] = keep_own_segment(qm_ref[0]).astype(qbm_scr.dtype)
        qbd_scr[...] = keep_own_segment(qd_ref[0]).astype(qbd_scr.dtype)
        gate_scr[...] = jnp.full(gate_scr.shape, NEG_INF, F32)
        far = jnp.full((rows, 1), REL_MAX_DIST, jnp.int32)
        far_scr[0] = bias_rows(far, moba_cols)
        far_scr[1] = bias_rows(far, diff_cols)

    @pl.when(ph == 0)
    def _gate():
        s = jnp.dot(qbm_scr[...], kt_step(mk_pages), preferred_element_type=F32)
        col = lax.broadcasted_iota(jnp.int32, (rows, LANES), 1)
        g = gate_scr[...]
        for c in range(bps):
            bs = jnp.sum(s[:, c * MOBA_BLOCK:(c + 1) * MOBA_BLOCK], axis=-1, keepdims=True)
            g = jnp.where(col == j * bps + c, bs, g)
        gate_scr[...] = g

        @pl.when(j == n_steps - 1)
        def _select():
            sel = _top_select(g, col, 1, min(MOBA_TOPK, n_past), jnp.full((rows, 1), n_past, jnp.int32))
            sel_scr[...] = sel.astype(sel_scr.dtype)
            m_scr[...] = jnp.full(m_scr.shape, NEG_INF, F32)
            l_scr[...] = jnp.zeros(l_scr.shape, F32)
            accm_scr[...] = jnp.zeros(accm_scr.shape, F32)
            accd_scr[...] = jnp.zeros(accd_scr.shape, F32)

    def softmax_step(branch, s):
        m_prev = m_scr[branch]
        m_new = jnp.maximum(m_prev, jnp.max(s, axis=-1, keepdims=True))
        m_safe = jnp.where(m_new == NEG_INF, 0.0, m_new)
        alpha = jnp.exp2(m_prev - m_safe)
        pexp = jnp.exp2(s - m_safe)
        l_scr[branch] = alpha * l_scr[branch] + jnp.sum(pexp, axis=-1, keepdims=True)
        m_scr[branch] = m_new
        return alpha, pexp.astype(MM)

    def diff_pv(alpha, pexp, v_of_head):
        for h in range(DIFF_HEADS):
            rs = slice(h * 2 * TOKEN_ROWS, (h + 1) * 2 * TOKEN_ROWS)
            accd_scr[rs] = alpha[rs] * accd_scr[rs] + jnp.dot(pexp[rs], v_of_head(h), preferred_element_type=F32)

    @pl.when(ph == 1)
    def _attend():
        kpos = j * tk + lax.broadcasted_iota(jnp.int32, (rows, tk), 1)
        rel = past_len + row_t - kpos
        blk_of_col = (lax.broadcasted_iota(jnp.int32, (LANES, tk), 1) // MOBA_BLOCK) + j * bps
        onehot = (lax.broadcasted_iota(jnp.int32, (LANES, tk), 0) == blk_of_col).astype(sel_scr.dtype)
        allowed = jnp.dot(sel_scr[...], onehot, preferred_element_type=F32)
        s_m = jnp.dot(qbm_scr[...], kt_step(mk_pages), preferred_element_type=F32) + page_bias(0, rel, moba_cols)
        alpha, pexp = softmax_step(0, jnp.where(allowed > 0.5, s_m, NEG_INF))
        accm_scr[...] = alpha * accm_scr[...] + _dot_nt(pexp, kt_step(mv_pages))
        s_d = jnp.dot(qbd_scr[...], kt_step(dk_pages), preferred_element_type=F32) + page_bias(1, rel, diff_cols)
        alpha, pexp = softmax_step(1, s_d)
        diff_pv(alpha, pexp, lambda h: jnp.concatenate(
            [r[0, 0, pl.ds(h, PAGE_ROWS, stride=DIFF_HEADS), :] for r in dv_pages], axis=0).astype(MM))

        @pl.when(j == n_steps - 1)
        def _finish():
            ncol = kmn_ref.shape[1]
            col = lax.broadcasted_iota(jnp.int32, (rows, ncol), 1)
            reln = row_t - (col - b * tn)
            ok = (col >= b * tn) & (col < (b + 1) * tn) & (reln >= 0)
            s_mn = _dot(qbm_scr[...], kmn_ref[...]) + bias_rows(reln, moba_cols)
            alpha, pexp = softmax_step(0, jnp.where(ok, s_mn, NEG_INF))
            accm = alpha * accm_scr[...] + _dot_nt(pexp, vmn_ref[...])
            s_dn = _dot(qbd_scr[...], kdn_ref[...]) + bias_rows(reln, diff_cols)
            alpha, pexp = softmax_step(1, jnp.where(ok, s_dn, NEG_INF))
            diff_pv(alpha, pexp, lambda h: vdn_ref[:, h * DIFF_VDIM:(h + 1) * DIFF_VDIM].astype(MM))
            lam = lam_smem[0]
            om = accm / l_scr[0]
            for h in range(MOBA_HEADS):
                om_ref[0, :, h * HEAD_DIM:(h + 1) * HEAD_DIM] = (
                    om[h * TOKEN_ROWS:h * TOKEN_ROWS + tn, h * HEAD_DIM:(h + 1) * HEAD_DIM])
            od = accd_scr[...] / l_scr[1]
            for h in range(DIFF_HEADS):
                o0 = od[2 * h * TOKEN_ROWS:2 * h * TOKEN_ROWS + tn]
                o1 = od[(2 * h + 1) * TOKEN_ROWS:(2 * h + 1) * TOKEN_ROWS + tn]
                od_ref[0, :, h * DIFF_VDIM:(h + 1) * DIFF_VDIM] = o0 - lam * o1


def _sample_attn(qm, qd, kmn, vmn, kdn, vdn, caches, page_table, layer, rel_bias, lam, tn):
    nb = qm.shape[0]
    n_pages = page_table.shape[1]
    n_steps = n_pages // PAGES_PER_STEP
    ncol = kmn.shape[1]
    qspec = pl.BlockSpec((1, SROWS, MOBA_W), lambda b, ph, j, pt: (b, 0, 0))
    newt = pl.BlockSpec((MOBA_W, ncol), lambda b, ph, j, pt: (0, 0))
    newr = pl.BlockSpec((ncol, DIFF_W), lambda b, ph, j, pt: (0, 0))
    out = pl.BlockSpec((1, tn, MOBA_W), lambda b, ph, j, pt: (b, 0, 0))
    smem = pl.BlockSpec(memory_space=pltpu.SMEM)

    def page_specs(cache, both_phases):
        zeros = (0,) * (cache.ndim - 2)

        def index_map(b, ph, j, pt, g):
            step = j if both_phases else j * ph
            return (layer, pt[b, step * PAGES_PER_STEP + g]) + zeros
        return [pl.BlockSpec((1, 1) + cache.shape[2:], functools.partial(index_map, g=g))
                for g in range(PAGES_PER_STEP)]

    mk, mv, dk, dv = caches
    grid_spec = pltpu.PrefetchScalarGridSpec(
        num_scalar_prefetch=1, grid=(nb, 2, n_steps),
        in_specs=([smem, smem, qspec, qspec, newt, newt, newt, newr]
                  + page_specs(mk, True) + page_specs(mv, False) + page_specs(dk, False) + page_specs(dv, False)),
        out_specs=[out, out],
        scratch_shapes=[
            pltpu.VMEM((SROWS, MOBA_W), MM), pltpu.VMEM((SROWS, MOBA_W), MM),
            pltpu.VMEM((SROWS, LANES), F32), pltpu.VMEM((SROWS, LANES), MM),
            pltpu.VMEM((2, SROWS, 1), F32), pltpu.VMEM((2, SROWS, 1), F32), pltpu.VMEM((2, SROWS, 1), F32),
            pltpu.VMEM((SROWS, MOBA_W), F32), pltpu.VMEM((SROWS, DIFF_VDIM), F32),
        ])
    args = []
    for c in caches:
        args += [c] * PAGES_PER_STEP
    return pl.pallas_call(
        functools.partial(_sattn_kernel, tn=tn), grid_spec=grid_spec,
        out_shape=[jax.ShapeDtypeStruct((nb, tn, MOBA_W), F32), jax.ShapeDtypeStruct((nb, tn, DIFF_W), F32)],
        compiler_params=_cparams("arbitrary", "arbitrary", "arbitrary"), name="sattn",
    )(page_table, rel_bias, lam, qm, qd, kmn, vmn, kdn, vdn, *args)


def _merge_kernel(sub_scale, h_ref, om_ref, od_ref, gsub_ref, wout_ref, gffn_ref, wpq_ref, h1_ref, xn_ref, q_ref):
    od = od_ref[...]
    parts = [om_ref[...]]
    for h in range(DIFF_HEADS):
        parts.append(_rms(od[:, h * DIFF_VDIM:(h + 1) * DIFF_VDIM], gsub_ref[...]) * sub_scale)
    y = jnp.concatenate(parts, axis=-1)
    h1 = h_ref[...] + _dot(y, wout_ref[...])
    h1_ref[...] = h1
    xn = _rms(h1, gffn_ref[...])
    xn_ref[...] = xn.astype(xn_ref.dtype)
    q_ref[...] = _dot(xn, wpq_ref[...]).astype(q_ref.dtype)


def _merge(h, om, od, g_sub, w_out, g_ffn, w_pq, sub_scale, tt):
    t, d = h.shape
    nq = w_pq.shape[1]
    row = lambda i: (i, 0)
    const = lambda i: (0, 0)
    return pl.pallas_call(
        functools.partial(_merge_kernel, sub_scale),
        grid=(t // tt,),
        in_specs=[pl.BlockSpec((tt, d), row), pl.BlockSpec((tt, MOBA_W), row), pl.BlockSpec((tt, DIFF_W), row),
                  pl.BlockSpec((1, DIFF_VDIM), const), pl.BlockSpec(w_out.shape, const),
                  pl.BlockSpec((1, d), const), pl.BlockSpec(w_pq.shape, const)],
        out_specs=[pl.BlockSpec((tt, d), row), pl.BlockSpec((tt, d), row), pl.BlockSpec((tt, nq), row)],
        out_shape=[jax.ShapeDtypeStruct((t, d), F32), jax.ShapeDtypeStruct((t, d), MM),
                   jax.ShapeDtypeStruct((t, nq), MM)],
        compiler_params=_cparams("parallel"), name="merge",
    )(h, om, od, g_sub, w_out, g_ffn, w_pq)


def _topk_rows(s, k, want_rank):
    n = s.shape[0]
    row = lax.broadcasted_iota(jnp.int32, s.shape, 0)
    rank = jnp.full(s.shape, k, jnp.int32) if want_rank else None
    vals = jnp.zeros((k, s.shape[1]), F32)
    idxs = jnp.zeros((k, s.shape[1]), jnp.int32)
    vrow = lax.broadcasted_iota(jnp.int32, vals.shape, 0)
    for r in range(k):
        m = jnp.max(s, axis=0, keepdims=True)
        idx = jnp.min(jnp.where(s == m, row, n), axis=0, keepdims=True)
        hit = row == idx
        if want_rank:
            rank = jnp.where(hit, r, rank)
        s = jnp.where(hit, NEG_INF, s)
        vals = jnp.where(vrow == r, m, vals)
        idxs = jnp.where(vrow == r, idx, idxs)
    return vals, idxs, rank


def _top_pair_sums(v1, v2, k):
    row = lax.broadcasted_iota(jnp.int32, v1.shape, 0)
    p = jnp.zeros(v1.shape, jnp.int32)
    g = jnp.broadcast_to(v2[0:1], v1.shape)
    top = v1[0:1] + v2[0:1]
    z = jnp.zeros(top.shape, F32)
    for _ in range(k):
        f = v1 + g
        m = jnp.max(f, axis=0, keepdims=True)
        a = jnp.min(jnp.where(f == m, row, k), axis=0, keepdims=True)
        hit = row == a
        z = z + jnp.exp(m - top)
        p = jnp.where(hit, p + 1, p)
        nxt = jnp.max(jnp.where(hit, p, -1), axis=0, keepdims=True)
        v_nxt = jnp.max(jnp.where(row == nxt, v2, NEG_INF), axis=0, keepdims=True)
        g = jnp.where(hit, v_nxt, g)
    return p, z


def _peersel_kernel(q_ref, keys_ref, rank2_ref, cnt_ref, e1_ref, e2_ref):
    k = PEER_TOPK
    q = q_ref[...]
    s1 = _dot_nt(keys_ref[0, 0], q[:, :PEER_HALF])
    s2 = _dot_nt(keys_ref[0, 1], q[:, PEER_HALF:])
    v1, idx1, _ = _topk_rows(s1, k, False)
    v2, _, rank2 = _topk_rows(s2, k, True)
    picks, z = _top_pair_sums(v1, v2, k)
    key = lax.broadcasted_iota(jnp.int32, s1.shape, 0)
    cnt = jnp.zeros(s1.shape, jnp.int32)
    for r in range(k):
        cnt = jnp.where(key == idx1[r:r + 1], picks[r:r + 1], cnt)
    rank2_ref[0] = rank2.astype(rank2_ref.dtype)
    cnt_ref[0] = cnt.astype(cnt_ref.dtype)
    e1_ref[0] = (jnp.exp(s1 - v1[0:1]) / z).astype(e1_ref.dtype)
    e2_ref[0] = jnp.exp(s2 - v2[0:1]).astype(e2_ref.dtype)


def _peersel(q, sub_keys, tt):
    t = q.shape[0]
    spec = pl.BlockSpec((1, PEER_NKEYS, tt), lambda i, h: (h, 0, i))
    tile = jax.ShapeDtypeStruct((PEER_HEADS, PEER_NKEYS, t), MM)
    rows = jax.ShapeDtypeStruct((PEER_HEADS, PEER_NKEYS, t), F32)
    return pl.pallas_call(
        _peersel_kernel,
        grid=(t // tt, PEER_HEADS),
        in_specs=[pl.BlockSpec((tt, 2 * PEER_HALF), lambda i, h: (i, h)),
                  pl.BlockSpec((1, 2, PEER_NKEYS, PEER_HALF), lambda i, h: (h, 0, 0, 0))],
        out_specs=[spec] * 4, out_shape=[tile, rows, rows, tile],
        compiler_params=_cparams("parallel", "arbitrary"), name="peersel",
    )(q, sub_keys)


PEER_I1_PER_STEP = SUBLANES
PEER_TOKEN_SPLIT = 2
PEER_TOKEN_TILE = 512


def _peerffn_kernel(xn_ref, u_ref, vt_ref, rank2_ref, cnt_ref, e1_ref, e2_ref, out_ref, hid_scr, coef_scr, acc_scr):
    n_split = hid_scr.shape[0]
    j = pl.program_id(1)

    @pl.when(j == 0)
    def _zero():
        acc_scr[...] = jnp.zeros(acc_scr.shape, F32)

    tt = xn_ref.shape[0]
    th = tt // n_split
    wdt = rank2_ref.dtype
    pack = 4 // jnp.dtype(wdt).itemsize * SUBLANES
    grp = (PEER_NKEYS // pack, pack, th)

    def hidden(hf):
        hid_scr[hf] = _dot_nt(u_ref[...], xn_ref[hf * th:(hf + 1) * th, :])

    def weights(hf):
        ts = slice(hf * th, (hf + 1) * th)
        for a in range(PEER_I1_PER_STEP):
            rs = slice(a * PEER_NKEYS, (a + 1) * PEER_NKEYS)
            hid = hid_scr[hf, rs]
            act = 0.5 * hid * (1.0 + lax.erf(hid * (2.0 ** -0.5)))
            w = jnp.zeros(grp, wdt)
            for h in range(PEER_HEADS):
                cnt = jnp.broadcast_to(cnt_ref[h, a:a + 1, ts], (pack, th)).astype(wdt)
                e1 = jnp.broadcast_to(e1_ref[h, a:a + 1, ts], (pack, th)).astype(wdt)
                w = w + jnp.where(rank2_ref[h, :, ts].reshape(grp) < cnt, e2_ref[h, :, ts].reshape(grp),
                                  jnp.zeros_like(w)) * e1
            coef_scr[hf, rs] = w.reshape(PEER_NKEYS, th) * act.astype(wdt)

    def project(hf):
        ts = slice(hf * th, (hf + 1) * th)
        acc_scr[:, ts] += jnp.dot(vt_ref[...], coef_scr[hf], preferred_element_type=F32)

    hidden(0)
    for hf in range(n_split):
        if hf + 1 < n_split:
            hidden(hf + 1)
        weights(hf)
        project(hf)

    @pl.when(j == pl.num_programs(1) - 1)
    def _store():
        out_ref[...] = acc_scr[...].T


def _peerffn(xn, u, vt, rank2, cnt, e1, e2, tt):
    t, d = xn.shape
    ne = PEER_I1_PER_STEP * PEER_NKEYS
    n_split = PEER_TOKEN_SPLIT if tt % (PEER_TOKEN_SPLIT * LANES) == 0 else 1
    th = tt // n_split
    tile = pl.BlockSpec((PEER_HEADS, PEER_NKEYS, tt), lambda i, j: (0, 0, i))
    rows = pl.BlockSpec((PEER_HEADS, PEER_I1_PER_STEP, tt), lambda i, j: (0, j, i))
    return pl.pallas_call(
        _peerffn_kernel,
        grid=(t // tt, PEER_NKEYS // PEER_I1_PER_STEP),
        in_specs=[pl.BlockSpec((tt, d), lambda i, j: (i, 0)),
                  pl.BlockSpec((ne, d), lambda i, j: (j, 0)), pl.BlockSpec((d, ne), lambda i, j: (0, j)),
                  tile, rows, rows, tile],
        out_specs=pl.BlockSpec((tt, d), lambda i, j: (i, 0)),
        out_shape=jax.ShapeDtypeStruct((t, d), F32),
        scratch_shapes=[pltpu.VMEM((n_split, ne, th), F32), pltpu.VMEM((n_split, ne, th), rank2.dtype),
                        pltpu.VMEM((d, tt), F32)],
        compiler_params=_cparams("parallel", "arbitrary"), name="peerffn",
    )(xn, u, vt, rank2, cnt, e1, e2)


def _ple_kernel(h_ref, peer_ref, p_ref, g_ref, wg_ref, wp_ref, out_ref):
    h2 = h_ref[...] + peer_ref[...]
    gate = jax.nn.sigmoid(_dot(_rms(h2, g_ref[...]), wg_ref[...]))
    out_ref[...] = h2 + gate * _dot(p_ref[...], wp_ref[...])


def _ple(h1, peer, p, g_ple, w_pg, w_pp, tt):
    t, d = h1.shape
    row = lambda i: (i, 0)
    const = lambda i: (0, 0)
    return pl.pallas_call(
        _ple_kernel,
        grid=(t // tt,),
        in_specs=[pl.BlockSpec((tt, d), row), pl.BlockSpec((tt, d), row), pl.BlockSpec((tt, p.shape[1]), row),
                  pl.BlockSpec((1, d), const), pl.BlockSpec(w_pg.shape, const), pl.BlockSpec(w_pp.shape, const)],
        out_specs=pl.BlockSpec((tt, d), row),
        out_shape=jax.ShapeDtypeStruct((t, d), F32),
        compiler_params=_cparams("parallel"), name="ple",
    )(h1, peer, p, g_ple, w_pg, w_pp)


def _token_tile(t, pref):
    return pref if t % pref == 0 else t


def kernel(x_prompt, x_sample, cache_moba_k, cache_moba_v, cache_diff_k, cache_diff_v, page_table, p_prompt, p_sample, rel_bias, g_attn, w_in, g_q_moba, g_k_moba, g_q_diff, g_k_diff, lam_q1, lam_k1, lam_q2, lam_k2, g_subln, w_out, g_ffn, w_peer_q, peer_sub_keys, peer_u, peer_v, g_ple, w_ple_gate, w_ple_proj):
    bsz, seq, d = x_prompt.shape
    nb, tn, _ = x_sample.shape
    depth = w_in.shape[0]
    page = cache_moba_k.shape[2]
    assert page == PAGE_ROWS and seq % MOBA_BLOCK == 0 and tn <= TOKEN_ROWS and REL_MAX_DIST <= PAGE_ROWS
    past_len = page_table.shape[1] * page
    assert past_len % (PAGES_PER_STEP * page) == 0 and past_len // MOBA_BLOCK <= LANES

    seg = jnp.asarray(np.kron(np.eye(MOBA_W // HEAD_DIM), np.full((HEAD_DIM, HEAD_DIM), 1.0 / HEAD_DIM)), jnp.bfloat16)
    caches = (jnp.transpose(cache_moba_k, (0, 1, 3, 4, 2)), jnp.transpose(cache_moba_v, (0, 1, 3, 4, 2)),
              jnp.transpose(cache_diff_k, (0, 1, 3, 4, 5, 2)),
              cache_diff_v.reshape(depth, cache_diff_v.shape[1], page * DIFF_HEADS, DIFF_VDIM))
    tile8 = lambda g: jnp.tile(g, MOBA_W // HEAD_DIM)

    hp = x_prompt.reshape(bsz * seq, d)
    hs = x_sample.reshape(nb * tn, d)
    new_p = [[] for _ in range(4)]
    new_s = [[] for _ in range(4)]
    w0, w1, w2, w3, w4, w5 = (slice(c * MOBA_W, (c + 1) * MOBA_W) for c in range(6))
    for i in range(depth):
        lam_init = 0.8 - 0.6 * math.exp(-0.3 * i)
        lam = (jnp.exp(jnp.sum(lam_q1[i] * lam_k1[i])) - jnp.exp(jnp.sum(lam_q2[i] * lam_k2[i])) + lam_init)
        lam = lam.reshape(1).astype(F32)
        wi = w_in[i]
        wkv = jnp.concatenate([wi[:, w1], wi[:, w4], wi[:, w2], wi[:, w5]], axis=1).astype(MM)
        wqt = jnp.concatenate([wi[:, w0], wi[:, w3]], axis=1).T.astype(MM)
        wvt = jnp.concatenate([wi[:, w2], wi[:, w5]], axis=1).T.astype(MM)
        gk = jnp.concatenate([tile8(g_k_moba[i]), tile8(g_k_diff[i])])[None, :]
        gqt = (jnp.concatenate([tile8(g_q_moba[i]), tile8(g_q_diff[i])]) * (ATTN_SCALE * LOG2E))[:, None]
        w_out_i = w_out[i].astype(MM)
        w_pq_i = w_peer_q[i].astype(MM)
        u_i = peer_u[i].astype(MM)
        vt_i = peer_v[i].T.astype(MM)
        w_pg_i = w_ple_gate[i].astype(MM)
        w_pp_i = w_ple_proj[i].astype(MM)

        tp = _token_tile(bsz * seq, 512)
        kmt, vmt, kdt, vd, kb, qt, vt, kmean = _proj(hp, g_attn[i][None], wkv, wqt, wvt, seg, gk, gqt, tp, bsz, True)
        for lst, a in zip(new_p, (kmt, vmt, kdt, vd)):
            lst.append(a)
        kmean = kmean.reshape(bsz, seq // MOBA_BLOCK, MOBA_W)
        kmean = jnp.pad(kmean, ((0, 0), (0, -kmean.shape[1] % SUBLANES), (0, 0)))
        om_p, od_p = _prompt_attn(qt, kb, vt, kmean, rel_bias, lam, bsz)

        ts = _token_tile(nb * tn, 128)
        kmt, vmt, kdt, vd, _, qt, _ = _proj(hs, g_attn[i][None], wkv, wqt, wvt, seg, gk, gqt, ts, 1, False)
        for lst, a in zip(new_s, (kmt, vmt, kdt, vd)):
            lst.append(a)
        qs = qt.T.astype(F32).reshape(nb, 1, tn, QK_W)
        qs = jnp.pad(qs, ((0, 0), (0, 0), (0, TOKEN_ROWS - tn), (0, 0)))
        qs = jnp.broadcast_to(qs, (nb, MAPS_PER_TOKEN, TOKEN_ROWS, QK_W)).reshape(nb, SROWS, QK_W)
        om_s, od_s = _sample_attn(qs[:, :, :MOBA_W], qs[:, :, MOBA_W:], kmt[0], vmt[0], kdt[0], vd, caches,
                                  page_table, i, rel_bias, lam, tn)

        outs = []
        for h, om, od, p, tt in ((hp, om_p, od_p, p_prompt[i].reshape(bsz * seq, -1), tp),
                                 (hs, om_s.reshape(-1, MOBA_W), od_s.reshape(-1, DIFF_W),
                                  p_sample[i].reshape(nb * tn, -1), ts)):
            h1, xn, q = _merge(h, om, od, g_subln[i][None], w_out_i, g_ffn[i][None], w_pq_i, 1.0 - lam_init, tt)
            rank2, cnt, e1, e2 = _peersel(q, peer_sub_keys[i], tt)
            peer = _peerffn(xn, u_i, vt_i, rank2, cnt, e1, e2, _token_tile(xn.shape[0], PEER_TOKEN_TILE))
            outs.append(_ple(h1, peer, p, g_ple[i][None], w_pg_i, w_pp_i, tt))
        hp, hs = outs

    def stack(lst, lead, tail):
        def rows(a):
            if a.ndim == 3:
                a = jnp.moveaxis(a.reshape(a.shape[0], *tail, a.shape[2]), -1, 1)
            return a.reshape(*lead, *tail)
        return jnp.stack([rows(a) for a in lst])

    lp, ls = (bsz, seq), (nb, tn)
    tails = ((MOBA_HEADS, HEAD_DIM), (MOBA_HEADS, HEAD_DIM), (DIFF_HEADS, 2, HEAD_DIM), (DIFF_HEADS, DIFF_VDIM))
    return (hp.reshape(bsz, seq, d), hs.reshape(nb, tn, d),
            *[stack(l, lp, t) for l, t in zip(new_p, tails)],
            *[stack(l, ls, t) for l, t in zip(new_s, tails)])
```
